```python
import math, functools
import jax, jax.numpy as jnp
from jax import lax
import numpy as np

D_MODEL = 2048
BATCH = 8
SEQ = 2048
DEPTH = 1
DEC_BATCH = 32
DEC_SEQ = 1
PAST_LEN = 16384
PAGE_SIZE = 128

HEAD_DIM = 128
N_FOX_HEADS = D_MODEL // (2 * HEAD_DIM)
N_DIFF_HEADS = D_MODEL // (2 * HEAD_DIM)
DIFF_QK_DIM = HEAD_DIM // 2
FOX_W = N_FOX_HEADS * HEAD_DIM
DIFF_W = N_DIFF_HEADS * HEAD_DIM
MIX_WIDTH = FOX_W + DIFF_W
IN_WIDTH = 3 * FOX_W + N_FOX_HEADS + 3 * DIFF_W
N_MEM = 256
N_XATTN_HEADS = 4
XATTN_HEAD_DIM = 128
XATTN_W = N_XATTN_HEADS * XATTN_HEAD_DIM
N_EXPERTS = 32
TOP_K = 4
D_FF = D_MODEL
SWIGLU_LIMIT = 7.0
SWIGLU_ALPHA = 1.702
Q_BLOCK = 128
MOE_BLOCK = 128
NORM_EPS = 1e-5
F32 = jnp.float32

kernel_name = 'hymba_fox_diffattn_alibi_moe_step'


def rmsnorm(x, g):
    xf = x.astype(F32)
    y = xf * lax.rsqrt(jnp.mean(xf * xf, axis=-1, keepdims=True) + NORM_EPS)
    return (y * g.astype(F32)).astype(x.dtype)


def alibi_slopes(n_heads):
    return jnp.asarray([2.0 ** (-8.0 * (h + 1) / n_heads) for h in range(n_heads)], dtype=F32)


def diff_lambda_init(layer):
    return 0.8 - 0.6 * math.exp(-0.3 * layer)


def diff_lambda(lq1, lk1, lq2, lk2, lam_init):
    dot_exp = lambda a, b: jnp.exp(jnp.sum(a.astype(F32) * b.astype(F32)))
    return dot_exp(lq1, lk1) - dot_exp(lq2, lk2) + lam_init


def gather_pages(cache, layer, page_table):
    rows = cache[layer, page_table]
    return rows.reshape(rows.shape[0], rows.shape[1] * rows.shape[2], *rows.shape[3:])


def project_mix(h, w_in, b_forget):
    b, t, _ = h.shape
    sizes = [FOX_W] * 3 + [N_FOX_HEADS] + [DIFF_W] * 3
    cuts = [int(c) for c in np.cumsum(sizes)[:-1]]
    fq, fk, fv, fg, dq, dk, dv = jnp.split(h @ w_in, cuts, axis=-1)
    heads = lambda a, n: a.reshape(b, t, n, HEAD_DIM)
    logf = jax.nn.log_sigmoid(fg.astype(F32) + b_forget.astype(F32))
    return (heads(fq, N_FOX_HEADS), heads(fk, N_FOX_HEADS), heads(fv, N_FOX_HEADS), logf,
            heads(dq, N_DIFF_HEADS), heads(dk, N_DIFF_HEADS), heads(dv, N_DIFF_HEADS))


def scores(q, ks, scale):
    return jnp.concatenate([jnp.einsum('bqhd,bkhd->bhqk', q, k).astype(F32) for k in ks], axis=-1) * scale


def weigh_values(p, vs):
    outs, start = [], 0
    for v in vs:
        n = v.shape[1]
        outs.append(jnp.einsum('bhqk,bkhd->bqhd', p[..., start:start + n].astype(v.dtype), v))
        start += n
    return functools.reduce(jnp.add, outs)


def fox_core(q, ks, vs, d_q, d_k, q_pos, k_pos):
    decay = jnp.swapaxes(d_q, 1, 2)[..., :, None] - jnp.swapaxes(d_k, 1, 2)[..., None, :]
    logits = scores(q, ks, HEAD_DIM ** -0.5) + decay
    causal = k_pos[None, :] <= q_pos[:, None]
    p = jax.nn.softmax(jnp.where(causal, logits, -jnp.inf), axis=-1)
    return weigh_values(p, vs)


def diff_core(q, ks, vs, q_pos, k_pos, lam, g_sub, lam_init):
    slopes = alibi_slopes(N_DIFF_HEADS)
    alibi = -slopes[:, None, None] * (q_pos[:, None] - k_pos[None, :]).astype(F32)
    causal = k_pos[None, :] <= q_pos[:, None]
    first_half = jnp.arange(HEAD_DIM) < DIFF_QK_DIM

    def attn_map(q_half):
        logits = scores(q_half, ks, DIFF_QK_DIM ** -0.5) + alibi
        return jax.nn.softmax(jnp.where(causal, logits, -jnp.inf), axis=-1)

    p = attn_map(jnp.where(first_half, q, 0)) - lam * attn_map(jnp.where(first_half, 0, q))
    o = weigh_values(p, vs)
    return rmsnorm(o, g_sub) * (1.0 - lam_init)


def over_query_blocks(fn, q_args, q_pos):
    nb = q_pos.shape[0] // Q_BLOCK
    split = lambda a: jnp.swapaxes(a.reshape(a.shape[0], nb, Q_BLOCK, *a.shape[2:]), 0, 1)
    blocks = tuple(split(a) for a in q_args)
    out = lax.map(lambda args: fn(*args[0], args[1]), (blocks, q_pos.reshape(nb, Q_BLOCK)))
    out = jnp.swapaxes(out, 0, 1)
    return out.reshape(out.shape[0], nb * Q_BLOCK, *out.shape[3:])


def merge_heads(fox_o, diff_o, w_out):
    b, t = fox_o.shape[:2]
    cat = jnp.concatenate([fox_o.reshape(b, t, FOX_W), diff_o.reshape(b, t, DIFF_W)], axis=-1)
    return cat @ w_out


def memory_kv(mem, g_mem, w_mem_kv):
    b, m, _ = mem.shape
    k, v = jnp.split(rmsnorm(mem, g_mem) @ w_mem_kv, 2, axis=-1)
    return (k.reshape(b, m, N_XATTN_HEADS, XATTN_HEAD_DIM), v.reshape(b, m, N_XATTN_HEADS, XATTN_HEAD_DIM))


def cross_attend(h, mk, mv, w_xq, w_xo):
    b, t, _ = h.shape
    q = (h @ w_xq).reshape(b, t, N_XATTN_HEADS, XATTN_HEAD_DIM)
    p = jax.nn.softmax(scores(q, (mk,), XATTN_HEAD_DIM ** -0.5), axis=-1)
    return weigh_values(p, (mv,)).reshape(b, t, XATTN_W) @ w_xo


def moe(h, w_router, b_router, w_gate_up, b_gate_up, w_down, b_down):
    shape = h.shape
    xf = h.reshape(-1, shape[-1])
    n = xf.shape[0]
    logits = (xf @ w_router).astype(F32) + b_router.astype(F32)
    top_logit, top_e = lax.top_k(logits, TOP_K)
    gate = jax.nn.softmax(top_logit, axis=-1).reshape(-1)
    expert = top_e.reshape(-1)
    n_assign = n * TOP_K
    token = jnp.arange(n_assign) // TOP_K
    n_blocks = -(-n_assign // MOE_BLOCK) + N_EXPERTS
    n_rows = n_blocks * MOE_BLOCK
    counts = jnp.bincount(expert, length=N_EXPERTS)
    padded = (counts + MOE_BLOCK - 1) // MOE_BLOCK * MOE_BLOCK
    pad_end = jnp.cumsum(padded)
    pad_start = pad_end - padded
    order = jnp.argsort(expert)
    sorted_e = expert[order]
    rank = jnp.arange(n_assign) - (jnp.cumsum(counts) - counts)[sorted_e]
    slot = pad_start[sorted_e] + rank
    row_token = jnp.zeros((n_rows,), jnp.int32).at[slot].set(token[order].astype(jnp.int32))
    row_gate = jnp.zeros((n_rows,), F32).at[slot].set(gate[order])
    block_expert = jnp.minimum(jnp.searchsorted(pad_end, jnp.arange(n_blocks) * MOE_BLOCK, side='right'),
                               N_EXPERTS - 1)

    def expert_block(args):
        rows, e = args
        gu = xf[rows] @ w_gate_up[e] + b_gate_up[e]
        g = jnp.minimum(gu[:, :D_FF], SWIGLU_LIMIT)
        u = jnp.clip(gu[:, D_FF:], -SWIGLU_LIMIT, SWIGLU_LIMIT)
        return ((u + 1.0) * (g * jax.nn.sigmoid(SWIGLU_ALPHA * g))) @ w_down[e] + b_down[e]

    out = lax.map(expert_block, (row_token.reshape(n_blocks, MOE_BLOCK), block_expert))
    out = out.reshape(n_rows, shape[-1]) * row_gate[:, None].astype(out.dtype)
    y = jnp.zeros_like(xf).at[row_token].add(out.astype(xf.dtype))
    return y.reshape(shape)


def setup_inputs(seed: int = 0) -> dict:
    key = jax.random.key(seed)
    keys = iter(jax.random.split(key, 48))

    def nrm(shape, scale=1.0):
        return jax.random.normal(next(keys), shape, F32) * scale

    def gain(shape):
        return 1.0 + nrm(shape, 0.01)

    n_pages = PAST_LEN // PAGE_SIZE
    n_pool = (5 * DEC_BATCH * n_pages + 3) // 4
    page_table = jax.random.permutation(next(keys), n_pool)[: DEC_BATCH * n_pages]
    page_table = page_table.reshape(DEC_BATCH, n_pages).astype(jnp.int32)
    paged = lambda n_heads: (DEPTH, n_pool, PAGE_SIZE, n_heads, HEAD_DIM)
    mem_shape = (DEPTH, DEC_BATCH, N_MEM, N_XATTN_HEADS, XATTN_HEAD_DIM)
    return {
        'x_prompt': nrm((BATCH, SEQ, D_MODEL)),
        'x_sample': nrm((DEC_BATCH, DEC_SEQ, D_MODEL)),
        'cache_fox_k': nrm(paged(N_FOX_HEADS)),
        'cache_fox_v': nrm(paged(N_FOX_HEADS)),
        'cache_fox_logf': jax.nn.log_sigmoid(nrm((DEPTH, n_pool, PAGE_SIZE, N_FOX_HEADS))),
        'cache_diff_k': nrm(paged(N_DIFF_HEADS)),
        'cache_diff_v': nrm(paged(N_DIFF_HEADS)),
        'cache_mem_k': nrm(mem_shape),
        'cache_mem_v': nrm(mem_shape),
        'page_table': page_table,
        'mem_prompt': nrm((BATCH, N_MEM, D_MODEL)),
        'g_mix': gain((DEPTH, D_MODEL)),
        'w_in': nrm((DEPTH, D_MODEL, IN_WIDTH), D_MODEL ** -0.5),
        'b_forget': nrm((DEPTH, N_FOX_HEADS), 0.1),
        'lambda_q1': nrm((DEPTH, DIFF_QK_DIM), 0.1),
        'lambda_k1': nrm((DEPTH, DIFF_QK_DIM), 0.1),
        'lambda_q2': nrm((DEPTH, DIFF_QK_DIM), 0.1),
        'lambda_k2': nrm((DEPTH, DIFF_QK_DIM), 0.1),
        'g_diff_sub': gain((DEPTH, HEAD_DIM)),
        'w_out': nrm((DEPTH, MIX_WIDTH, D_MODEL), MIX_WIDTH ** -0.5),
        'g_xattn': gain((DEPTH, D_MODEL)),
        'g_mem': gain((DEPTH, D_MODEL)),
        'w_mem_kv': nrm((DEPTH, D_MODEL, 2 * XATTN_W), D_MODEL ** -0.5),
        'w_xq': nrm((DEPTH, D_MODEL, XATTN_W), D_MODEL ** -0.5),
        'w_xo': nrm((DEPTH, XATTN_W, D_MODEL), XATTN_W ** -0.5),
        'g_ffn': gain((DEPTH, D_MODEL)),
        'w_router': nrm((DEPTH, D_MODEL, N_EXPERTS), D_MODEL ** -0.5),
        'b_router': nrm((DEPTH, N_EXPERTS), 0.01),
        'w_gate_up': nrm((DEPTH, N_EXPERTS, D_MODEL, 2 * D_FF), D_MODEL ** -0.5),
        'b_gate_up': nrm((DEPTH, N_EXPERTS, 2 * D_FF), 0.01),
        'w_down': nrm((DEPTH, N_EXPERTS, D_FF, D_MODEL), D_FF ** -0.5),
        'b_down': nrm((DEPTH, N_EXPERTS, D_MODEL), 0.01),
        'g_final': gain((D_MODEL,)),
    }


def reference(x_prompt, x_sample, cache_fox_k, cache_fox_v, cache_fox_logf, cache_diff_k, cache_diff_v,
              cache_mem_k, cache_mem_v, page_table, mem_prompt, g_mix, w_in, b_forget,
              lambda_q1, lambda_k1, lambda_q2, lambda_k2, g_diff_sub, w_out, g_xattn, g_mem, w_mem_kv,
              w_xq, w_xo, g_ffn, w_router, b_router, w_gate_up, b_gate_up, w_down, b_down, g_final):
    seq = x_prompt.shape[1]
    dec_seq = x_sample.shape[1]
    past = page_table.shape[1] * PAGE_SIZE
    pos_p = jnp.arange(seq)
    pos_q_s = past + jnp.arange(dec_seq)
    pos_k_s = jnp.arange(past + dec_seq)

    xp, xs = x_prompt, x_sample
    p_fk, p_fv, p_fl, p_dk, p_dv, p_mk, p_mv = [], [], [], [], [], [], []
    s_fk, s_fv, s_fl, s_dk, s_dv = [], [], [], [], []
    for l in range(DEPTH):
        lam0 = diff_lambda_init(l)
        lam = diff_lambda(lambda_q1[l], lambda_k1[l], lambda_q2[l], lambda_k2[l], lam0)

        h = rmsnorm(xp, g_mix[l])
        fq, fk, fv, logf, dq, dk, dv = project_mix(h, w_in[l], b_forget[l])
        dcum = jnp.cumsum(logf, axis=1)
        fox_o = over_query_blocks(
            lambda qb, db, pb: fox_core(qb, (fk,), (fv,), db, dcum, pb, pos_p), (fq, dcum), pos_p)
        diff_o = over_query_blocks(
            lambda qb, pb: diff_core(qb, (dk,), (dv,), pb, pos_p, lam, g_diff_sub[l], lam0), (dq,), pos_p)
        xp = xp + merge_heads(fox_o, diff_o, w_out[l])
        mk, mv = memory_kv(mem_prompt, g_mem[l], w_mem_kv[l])
        xp = xp + cross_attend(rmsnorm(xp, g_xattn[l]), mk, mv, w_xq[l], w_xo[l])
        xp = xp + moe(rmsnorm(xp, g_ffn[l]), w_router[l], b_router[l], w_gate_up[l], b_gate_up[l],
                      w_down[l], b_down[l])
        p_fk.append(fk); p_fv.append(fv); p_fl.append(logf); p_dk.append(dk); p_dv.append(dv)
        p_mk.append(mk); p_mv.append(mv)

        h = rmsnorm(xs, g_mix[l])
        sq, sk, sv, slogf, sdq, sdk, sdv = project_mix(h, w_in[l], b_forget[l])
        logf_all = jnp.concatenate([gather_pages(cache_fox_logf, l, page_table).astype(F32), slogf], axis=1)
        dcum_s = jnp.cumsum(logf_all, axis=1)
        fox_o = fox_core(sq, (gather_pages(cache_fox_k, l, page_table), sk),
                         (gather_pages(cache_fox_v, l, page_table), sv),
                         dcum_s[:, past:], dcum_s, pos_q_s, pos_k_s)
        diff_o = diff_core(sdq, (gather_pages(cache_diff_k, l, page_table), sdk),
                           (gather_pages(cache_diff_v, l, page_table), sdv),
                           pos_q_s, pos_k_s, lam, g_diff_sub[l], lam0)
        xs = xs + merge_heads(fox_o, diff_o, w_out[l])
        xs = xs + cross_attend(rmsnorm(xs, g_xattn[l]), cache_mem_k[l], cache_mem_v[l], w_xq[l], w_xo[l])
        xs = xs + moe(rmsnorm(xs, g_ffn[l]), w_router[l], b_router[l], w_gate_up[l], b_gate_up[l],
                      w_down[l], b_down[l])
        s_fk.append(sk); s_fv.append(sv); s_fl.append(slogf); s_dk.append(sdk); s_dv.append(sdv)

    y_prompt = rmsnorm(xp, g_final)
    y_sample = rmsnorm(xs, g_final)
    return (y_prompt, y_sample,
            jnp.stack(p_fk), jnp.stack(p_fv), jnp.stack(p_fl), jnp.stack(p_dk), jnp.stack(p_dv),
            jnp.stack(p_mk), jnp.stack(p_mv),
            jnp.stack(s_fk), jnp.stack(s_fv), jnp.stack(s_fl), jnp.stack(s_dk), jnp.stack(s_dv))
```

```python
import functools
import math

import jax
import jax.numpy as jnp
import numpy as np
from jax import lax
from jax.experimental import pallas as pl
from jax.experimental.pallas import tpu as pltpu

F32 = jnp.float32
BF16 = jnp.bfloat16
HIGHEST = lax.Precision.HIGHEST

HEAD_DIM = 128
DIFF_QK_DIM = 64
PAGE_SIZE = 128
N_XATTN_HEADS = 4
N_EXPERTS = 32
TOP_K = 4
SWIGLU_LIMIT = 7.0
SWIGLU_ALPHA = 1.702
NORM_EPS = 1e-5
LANES = 128
VMEM_LIMIT = 56 * 1024 * 1024
NEG_BIG = -1e30
MOE_TM = 256


def _cparams(*sem):
    return pltpu.CompilerParams(dimension_semantics=sem, vmem_limit_bytes=VMEM_LIMIT)


def _rms(x, g):
    return x * lax.rsqrt(jnp.mean(x * x, axis=-1, keepdims=True) + NORM_EPS) * g


def _dot(a, b, precise):
    if precise:
        return jnp.dot(a.astype(F32), b.astype(F32), preferred_element_type=F32, precision=HIGHEST)
    return jnp.dot(a.astype(BF16), b.astype(BF16), preferred_element_type=F32)


def _dot_nt(a, b, precise):
    dn = (((1,), (1,)), ((), ()))
    if precise:
        return lax.dot_general(a.astype(F32), b.astype(F32), dn, preferred_element_type=F32, precision=HIGHEST)
    return lax.dot_general(a.astype(BF16), b.astype(BF16), dn, preferred_element_type=F32)


def _proj_kernel(*refs, n_seg, tiles_per_seg, has_extra, precise):
    x_ref, g_ref, w_ref = refs[:3]
    pos = 3
    wx_ref = None
    if has_extra:
        wx_ref = refs[pos]
        pos += 1
    outs = refs[pos:pos + n_seg]
    pos += n_seg
    ox_ref = None
    if has_extra:
        ox_ref = refs[pos]
        pos += 1
    h_ref = refs[pos]
    j = pl.program_id(1)

    @pl.when(j == 0)
    def _():
        h = _rms(x_ref[...], g_ref[...])
        h_ref[...] = h.astype(h_ref.dtype)
        if has_extra:
            ox_ref[...] = _dot(h_ref[...], wx_ref[...], precise)

    acc = _dot(h_ref[...], w_ref[...], precise)
    for k in range(n_seg):
        @pl.when(j // tiles_per_seg == k)
        def _(k=k):
            outs[k][...] = acc.astype(outs[k].dtype)


def norm_project(x, g, w, seg_dtypes, seg_width, w_extra=None, *, tm, tn, precise=False):
    n, d = x.shape
    n_seg = len(seg_dtypes)
    assert w.shape == (d, n_seg * seg_width) and seg_width % tn == 0 and n % tm == 0
    tps = seg_width // tn
    has_extra = w_extra is not None
    in_specs = [pl.BlockSpec((tm, d), lambda i, j: (i, 0)),
                pl.BlockSpec((1, d), lambda i, j: (0, 0)),
                pl.BlockSpec((d, tn), lambda i, j: (0, j))]
    args = [x, g.reshape(1, d), w]
    if has_extra:
        in_specs.append(pl.BlockSpec((d, LANES), lambda i, j: (0, 0)))
        args.append(w_extra)
    out_specs = [pl.BlockSpec((tm, tn), functools.partial(
        lambda i, j, k: (i, jnp.clip(j - k * tps, 0, tps - 1)), k=k)) for k in range(n_seg)]
    out_shape = [jax.ShapeDtypeStruct((n, seg_width), dt) for dt in seg_dtypes]
    if has_extra:
        out_specs.append(pl.BlockSpec((tm, LANES), lambda i, j: (i, 0)))
        out_shape.append(jax.ShapeDtypeStruct((n, LANES), F32))
    return pl.pallas_call(
        functools.partial(_proj_kernel, n_seg=n_seg, tiles_per_seg=tps, has_extra=has_extra, precise=precise),
        grid=(n // tm, n_seg * tps),
        in_specs=in_specs, out_specs=out_specs, out_shape=out_shape,
        scratch_shapes=[pltpu.VMEM((tm, d), F32 if precise else BF16)],
        compiler_params=_cparams("parallel", "arbitrary"),
        name="norm_project",
    )(*args)


def _logf_kernel(fg_ref, b_ref, logf_ref, *rest, seq, blk, do_cumsum):
    x = fg_ref[0] + b_ref[...]
    lf = jnp.minimum(x, 0.0) - jnp.log1p(jnp.exp(-jnp.abs(x)))
    logf_ref[0] = lf
    if do_cumsum:
        dcum_ref = rest[0]
        r = lax.broadcasted_iota(jnp.int32, (blk, blk), 0)
        c = lax.broadcasted_iota(jnp.int32, (blk, blk), 1)
        tri = (c <= r).astype(F32)
        carry = jnp.zeros((1, LANES), F32)
        for i in range(seq // blk):
            part = jnp.dot(tri, lf[i * blk:(i + 1) * blk], preferred_element_type=F32, precision=HIGHEST) + carry
            dcum_ref[0, i * blk:(i + 1) * blk, :] = part
            carry = part[blk - 1:blk, :]


def logf_cumsum(fg, b_pad, do_cumsum):
    b, s, _ = fg.shape
    blk = min(s, 256)
    spec = pl.BlockSpec((1, s, LANES), lambda i: (i, 0, 0))
    n_out = 2 if do_cumsum else 1
    return pl.pallas_call(
        functools.partial(_logf_kernel, seq=s, blk=blk, do_cumsum=do_cumsum),
        grid=(b,),
        in_specs=[spec, pl.BlockSpec((1, LANES), lambda i: (0, 0))],
        out_specs=[spec] * n_out,
        out_shape=[jax.ShapeDtypeStruct((b, s, LANES), F32)] * n_out,
        compiler_params=_cparams("parallel"),
        name="logf_cumsum",
    )(fg, b_pad)


def _fox_prompt_kernel(q_ref, k_ref, v_ref, dcol_ref, drow_ref, o_ref, kb_ref, vb_ref, *, tq):
    qi = pl.program_id(2)

    @pl.when(qi == 0)
    def _():
        kb_ref[...] = k_ref[0].astype(BF16)
        vb_ref[...] = v_ref[0].astype(BF16)

    q = q_ref[0]
    dq = dcol_ref[0, 0]
    scale = HEAD_DIM ** -0.5

    def logits(kb):
        start = pl.multiple_of(kb * tq, tq)
        s = _dot_nt(q, kb_ref[pl.ds(start, tq), :], False) * scale
        return s + (dq - drow_ref[0, 0, kb]), start

    def update(s, start, m, l, acc):
        m_new = jnp.maximum(m, jnp.max(s, axis=-1, keepdims=True))
        alpha = jnp.exp(m - m_new)
        p = jnp.exp(s - m_new)
        l = alpha * l + jnp.sum(p, axis=-1, keepdims=True)
        acc = alpha * acc + jnp.dot(p.astype(BF16), vb_ref[pl.ds(start, tq), :], preferred_element_type=F32)
        return m_new, l, acc

    def body(kb, carry):
        s, start = logits(kb)
        return update(s, start, *carry)

    init = (jnp.full((tq, 1), NEG_BIG, F32), jnp.zeros((tq, 1), F32), jnp.zeros((tq, HEAD_DIM), F32))
    m, l, acc = lax.fori_loop(0, qi, body, init)
    s, start = logits(qi)
    r = lax.broadcasted_iota(jnp.int32, (tq, tq), 0)
    c = lax.broadcasted_iota(jnp.int32, (tq, tq), 1)
    s = jnp.where(c <= r, s, NEG_BIG)
    m, l, acc = update(s, start, m, l, acc)
    o_ref[0] = (acc / l).astype(o_ref.dtype)


def fox_prompt(fq, fk, fv, dcum, *, tq=256):
    b, s, w = fq.shape
    h = w // HEAD_DIM
    nq = s // tq
    dt = jnp.swapaxes(dcum[:, :, :h], 1, 2)
    dcol = dt.reshape(b, h, s, 1)
    drow = dt.reshape(b, h, nq, 1, tq)
    return pl.pallas_call(
        functools.partial(_fox_prompt_kernel, tq=tq),
        grid=(b, h, nq),
        in_specs=[pl.BlockSpec((1, tq, HEAD_DIM), lambda bi, hi, qi: (bi, qi, hi)),
                  pl.BlockSpec((1, s, HEAD_DIM), lambda bi, hi, qi: (bi, 0, hi)),
                  pl.BlockSpec((1, s, HEAD_DIM), lambda bi, hi, qi: (bi, 0, hi)),
                  pl.BlockSpec((1, 1, tq, 1), lambda bi, hi, qi: (bi, hi, qi, 0)),
                  pl.BlockSpec((1, 1, nq, 1, tq), lambda bi, hi, qi: (bi, hi, 0, 0, 0))],
        out_specs=pl.BlockSpec((1, tq, HEAD_DIM), lambda bi, hi, qi: (bi, qi, hi)),
        out_shape=jax.ShapeDtypeStruct((b, s, w), BF16),
        scratch_shapes=[pltpu.VMEM((s, HEAD_DIM), BF16), pltpu.VMEM((s, HEAD_DIM), BF16)],
        compiler_params=_cparams("parallel", "parallel", "arbitrary"),
        name="fox_prompt",
    )(fq, fk, fv, dcol, drow)


def _diff_lambda(lq1_ref, lk1_ref, lq2_ref, lk2_ref, lam_init):
    a = jnp.exp(jnp.sum(lq1_ref[...] * lk1_ref[...], axis=-1, keepdims=True))
    b = jnp.exp(jnp.sum(lq2_ref[...] * lk2_ref[...], axis=-1, keepdims=True))
    return a - b + lam_init


def _diff_finish(o1, o2, lam, gsub, lam_init):
    o = o1 - lam * o2
    return _rms(o, gsub) * (1.0 - lam_init)


def _diff_prompt_kernel(q_ref, k_ref, v_ref, slope_ref, lq1_ref, lk1_ref, lq2_ref, lk2_ref, gsub_ref,
                        o_ref, kb_ref, vb_ref, *, tq, lam_init):
    qi = pl.program_id(2)

    @pl.when(qi == 0)
    def _():
        kb_ref[...] = k_ref[0].astype(BF16)
        vb_ref[...] = v_ref[0].astype(BF16)

    q = q_ref[0]
    lane = lax.broadcasted_iota(jnp.int32, (tq, HEAD_DIM), 1)
    first = lane < DIFF_QK_DIM
    q1 = jnp.where(first, q, jnp.zeros_like(q))
    q2 = jnp.where(first, jnp.zeros_like(q), q)
    scale = DIFF_QK_DIM ** -0.5
    slope = slope_ref[0, :, :1]
    r = lax.broadcasted_iota(jnp.int32, (tq, tq), 0)
    c = lax.broadcasted_iota(jnp.int32, (tq, tq), 1)
    rel = (r - c).astype(F32)

    def step(kb, carry, masked):
        start = pl.multiple_of(kb * tq, tq)
        kblk = kb_ref[pl.ds(start, tq), :]
        vblk = vb_ref[pl.ds(start, tq), :]
        dist = rel + ((qi - kb) * tq).astype(F32)
        bias = -slope * dist
        out = []
        for qh, (m, l, acc) in zip((q1, q2), carry):
            s = _dot_nt(qh, kblk, False) * scale + bias
            if masked:
                s = jnp.where(c <= r, s, NEG_BIG)
            m_new = jnp.maximum(m, jnp.max(s, axis=-1, keepdims=True))
            alpha = jnp.exp(m - m_new)
            p = jnp.exp(s - m_new)
            l = alpha * l + jnp.sum(p, axis=-1, keepdims=True)
            acc = alpha * acc + jnp.dot(p.astype(BF16), vblk, preferred_element_type=F32)
            out.append((m_new, l, acc))
        return tuple(out)

    one = (jnp.full((tq, 1), NEG_BIG, F32), jnp.zeros((tq, 1), F32), jnp.zeros((tq, HEAD_DIM), F32))
    carry = lax.fori_loop(0, qi, lambda kb, cr: step(kb, cr, False), (one, one))
    (_, l1, a1), (_, l2, a2) = step(qi, carry, True)
    lam = _diff_lambda(lq1_ref, lk1_ref, lq2_ref, lk2_ref, lam_init)
    o_ref[0] = _diff_finish(a1 / l1, a2 / l2, lam, gsub_ref[...], lam_init).astype(o_ref.dtype)


def _alibi_slopes(n_heads):
    s = jnp.asarray([2.0 ** (-8.0 * (h + 1) / n_heads) for h in range(n_heads)], F32)
    return jnp.broadcast_to(s[:, None, None], (n_heads, 1, LANES))


def diff_prompt(dq, dk, dv, lams, gsub, lam_init, *, tq=256):
    b, s, w = dq.shape
    h = w // HEAD_DIM
    nq = s // tq
    small = lambda n: pl.BlockSpec((1, n), lambda bi, hi, qi: (0, 0))
    return pl.pallas_call(
        functools.partial(_diff_prompt_kernel, tq=tq, lam_init=lam_init),
        grid=(b, h, nq),
        in_specs=[pl.BlockSpec((1, tq, HEAD_DIM), lambda bi, hi, qi: (bi, qi, hi)),
                  pl.BlockSpec((1, s, HEAD_DIM), lambda bi, hi, qi: (bi, 0, hi)),
                  pl.BlockSpec((1, s, HEAD_DIM), lambda bi, hi, qi: (bi, 0, hi)),
                  pl.BlockSpec((1, 1, LANES), lambda bi, hi, qi: (hi, 0, 0)),
                  small(DIFF_QK_DIM), small(DIFF_QK_DIM), small(DIFF_QK_DIM), small(DIFF_QK_DIM),
                  small(HEAD_DIM)],
        out_specs=pl.BlockSpec((1, tq, HEAD_DIM), lambda bi, hi, qi: (bi, qi, hi)),
        out_shape=jax.ShapeDtypeStruct((b, s, w), BF16),
        scratch_shapes=[pltpu.VMEM((s, HEAD_DIM), BF16), pltpu.VMEM((s, HEAD_DIM), BF16)],
        compiler_params=_cparams("parallel", "parallel", "arbitrary"),
        name="diff_prompt",
    )(dq, dk, dv, _alibi_slopes(h), *lams, gsub)


def _outproj_kernel(x_ref, a1_ref, a2_ref, w1_ref, w2_ref, o_ref, *, precise):
    o_ref[...] = (x_ref[...] + _dot(a1_ref[...], w1_ref[...], precise)
                  + _dot(a2_ref[...], w2_ref[...], precise))


def out_project(x, a1, a2, w, *, tm, tn, precise=False):
    n, d = x.shape
    half = a1.shape[1]
    assert w.shape == (2 * half, d) and half % 8 == 0
    return pl.pallas_call(
        functools.partial(_outproj_kernel, precise=precise),
        grid=(n // tm, d // tn),
        in_specs=[pl.BlockSpec((tm, tn), lambda i, j: (i, j)),
                  pl.BlockSpec((tm, half), lambda i, j: (i, 0)),
                  pl.BlockSpec((tm, half), lambda i, j: (i, 0)),
                  pl.BlockSpec((half, tn), lambda i, j: (0, j)),
                  pl.BlockSpec((half, tn), lambda i, j: (1, j))],
        out_specs=pl.BlockSpec((tm, tn), lambda i, j: (i, j)),
        out_shape=jax.ShapeDtypeStruct((n, d), F32),
        compiler_params=_cparams("parallel", "arbitrary"),
        name="out_project",
    )(x, a1, a2, w, w)


def _xattn_kernel(x_ref, g_ref, wq_ref, mk_ref, mv_ref, wo_ref, o_ref, *, precise):
    x = x_ref[0]
    h = _rms(x, g_ref[...])
    q = _dot(h, wq_ref[...], precise)
    hd = q.shape[1] // N_XATTN_HEADS
    scale = hd ** -0.5
    outs = []
    for i in range(N_XATTN_HEADS):
        sl = slice(i * hd, (i + 1) * hd)
        s = _dot_nt(q[:, sl], mk_ref[0, :, sl], precise) * scale
        s = s - jnp.max(s, axis=-1, keepdims=True)
        p = jnp.exp(s)
        p = p / jnp.sum(p, axis=-1, keepdims=True)
        outs.append(_dot(p, mv_ref[0, :, sl], precise))
    o = jnp.concatenate(outs, axis=-1)
    o_ref[0] = x + _dot(o, wo_ref[...], precise)


def cross_attend(x, g, wq, mk, mv, wo, *, tm, precise=False):
    b, t, d = x.shape
    m, xw = mk.shape[1:]
    return pl.pallas_call(
        functools.partial(_xattn_kernel, precise=precise),
        grid=(b, t // tm),
        in_specs=[pl.BlockSpec((1, tm, d), lambda bi, i: (bi, i, 0)),
                  pl.BlockSpec((1, d), lambda bi, i: (0, 0)),
                  pl.BlockSpec((d, xw), lambda bi, i: (0, 0)),
                  pl.BlockSpec((1, m, xw), lambda bi, i: (bi, 0, 0)),
                  pl.BlockSpec((1, m, xw), lambda bi, i: (bi, 0, 0)),
                  pl.BlockSpec((xw, d), lambda bi, i: (0, 0))],
        out_specs=pl.BlockSpec((1, tm, d), lambda bi, i: (bi, i, 0)),
        out_shape=jax.ShapeDtypeStruct((b, t, d), F32),
        compiler_params=_cparams("parallel", "arbitrary"),
        name="cross_attend",
    )(x, g.reshape(1, d), wq, mk, mv, wo)


def _router_kernel(x_ref, g_ref, w_ref, b_ref, h_ref, e_ref, gate_ref, *, precise):
    h = _rms(x_ref[...], g_ref[...])
    h_ref[...] = h
    logits = _dot(h, w_ref[...], precise) + b_ref[...]
    tm = logits.shape[0]
    lane = lax.broadcasted_iota(jnp.int32, (tm, LANES), 1)
    logits = jnp.where(lane < N_EXPERTS, logits, -jnp.inf)
    e_out = jnp.zeros((tm, LANES), jnp.int32)
    v_out = jnp.full((tm, LANES), -jnp.inf, F32)
    for k in range(TOP_K):
        mx = jnp.max(logits, axis=-1, keepdims=True)
        idx = jnp.min(jnp.where(logits == mx, lane, LANES), axis=-1, keepdims=True)
        e_out = jnp.where(lane == k, idx, e_out)
        v_out = jnp.where(lane == k, mx, v_out)
        logits = jnp.where(lane == idx, -jnp.inf, logits)
    top = jnp.max(v_out, axis=-1, keepdims=True)
    p = jnp.exp(v_out - top)
    gate_ref[...] = p / jnp.sum(p, axis=-1, keepdims=True)
    e_ref[...] = e_out


def route(x, g, w_pad, b_pad, *, tm, precise=False):
    n, d = x.shape
    row = lambda c: pl.BlockSpec((tm, c), lambda i: (i, 0))
    return pl.pallas_call(
        functools.partial(_router_kernel, precise=precise),
        grid=(n // tm,),
        in_specs=[row(d), pl.BlockSpec((1, d), lambda i: (0, 0)),
                  pl.BlockSpec((d, LANES), lambda i: (0, 0)), pl.BlockSpec((1, LANES), lambda i: (0, 0))],
        out_specs=[row(d), row(LANES), row(LANES)],
        out_shape=[jax.ShapeDtypeStruct((n, d), F32), jax.ShapeDtypeStruct((n, LANES), jnp.int32),
                   jax.ShapeDtypeStruct((n, LANES), F32)],
        compiler_params=_cparams("parallel"),
        name="route",
    )(x, g.reshape(1, d), w_pad, b_pad)


def _gather_kernel(nu_ref, idx_ref, src_ref, o_ref, buf_ref, sem, *, tm):
    blk = pl.program_id(0)

    def row_copy(tok, r):
        return pltpu.make_async_copy(src_ref.at[pl.ds(tok, 1), :], buf_ref.at[pl.ds(r, 1), :], sem)

    @pl.when(blk < nu_ref[0])
    def _():
        def issue(r, c):
            row_copy(idx_ref[0, 0, r], r).start()
            return c

        def wait(r, c):
            row_copy(0, r).wait()
            return c

        lax.fori_loop(0, tm, issue, 0)
        lax.fori_loop(0, tm, wait, 0)
        o_ref[...] = buf_ref[...].astype(o_ref.dtype)

    @pl.when(blk >= nu_ref[0])
    def _():
        o_ref[...] = jnp.zeros_like(o_ref)


def moe_gather(h_all, row_token, n_used, *, tm):
    n, d = h_all.shape
    nb = row_token.shape[0] // tm
    return pl.pallas_call(
        functools.partial(_gather_kernel, tm=tm),
        grid_spec=pltpu.PrefetchScalarGridSpec(
            num_scalar_prefetch=1, grid=(nb,),
            in_specs=[pl.BlockSpec((1, 1, tm), lambda i, nu: (i, 0, 0), memory_space=pltpu.SMEM),
                      pl.BlockSpec(memory_space=pl.ANY)],
            out_specs=pl.BlockSpec((tm, d), lambda i, nu: (i, 0)),
            scratch_shapes=[pltpu.VMEM((tm, d), F32), pltpu.SemaphoreType.DMA]),
        out_shape=jax.ShapeDtypeStruct((nb * tm, d), BF16),
        compiler_params=_cparams("arbitrary"),
        name="moe_gather",
    )(n_used, row_token.reshape(nb, 1, tm), h_all)


def _moe_up_kernel(be_ref, nu_ref, x_ref, wg_ref, wu_ref, bg_ref, bu_ref, o_ref, wgb_ref, wub_ref):
    blk = pl.program_id(1)
    e = be_ref[blk]
    prev = be_ref[jnp.maximum(blk - 1, 0)]

    @pl.when((blk == 0) | (e != prev))
    def _():
        wgb_ref[...] = wg_ref[0].astype(BF16)
        wub_ref[...] = wu_ref[0].astype(BF16)

    @pl.when(blk < nu_ref[0])
    def _():
        x = x_ref[...]
        g = jnp.dot(x, wgb_ref[...], preferred_element_type=F32) + bg_ref[0]
        u = jnp.dot(x, wub_ref[...], preferred_element_type=F32) + bu_ref[0]
        g = jnp.minimum(g, SWIGLU_LIMIT)
        u = jnp.clip(u, -SWIGLU_LIMIT, SWIGLU_LIMIT)
        o_ref[...] = ((u + 1.0) * (g * jax.nn.sigmoid(SWIGLU_ALPHA * g))).astype(o_ref.dtype)

    @pl.when(blk >= nu_ref[0])
    def _():
        o_ref[...] = jnp.zeros_like(o_ref)


def moe_up(x_sorted, w_gate_up, b_gate_up, block_expert, n_used, *, tm, tn):
    r, d = x_sorted.shape
    ne, _, two_ff = w_gate_up.shape
    d_ff = two_ff // 2
    nb, nj = r // tm, d_ff // tn
    bias = b_gate_up.reshape(ne, 1, two_ff)
    return pl.pallas_call(
        _moe_up_kernel,
        grid_spec=pltpu.PrefetchScalarGridSpec(
            num_scalar_prefetch=2, grid=(nj, nb),
            in_specs=[pl.BlockSpec((tm, d), lambda j, i, be, nu: (i, 0)),
                      pl.BlockSpec((1, d, tn), lambda j, i, be, nu: (be[i], 0, j)),
                      pl.BlockSpec((1, d, tn), lambda j, i, be, nu: (be[i], 0, nj + j)),
                      pl.BlockSpec((1, 1, tn), lambda j, i, be, nu: (be[i], 0, j)),
                      pl.BlockSpec((1, 1, tn), lambda j, i, be, nu: (be[i], 0, nj + j))],
            out_specs=pl.BlockSpec((tm, tn), lambda j, i, be, nu: (i, j)),
            scratch_shapes=[pltpu.VMEM((d, tn), BF16), pltpu.VMEM((d, tn), BF16)]),
        out_shape=jax.ShapeDtypeStruct((r, d_ff), BF16),
        compiler_params=_cparams("arbitrary", "arbitrary"),
        name="moe_up",
    )(block_expert, n_used, x_sorted, w_gate_up, w_gate_up, bias, bias)


def _moe_down_kernel(be_ref, nu_ref, a_ref, w_ref, b_ref, o_ref, wb_ref):
    blk = pl.program_id(1)
    e = be_ref[blk]
    prev = be_ref[jnp.maximum(blk - 1, 0)]

    @pl.when((blk == 0) | (e != prev))
    def _():
        wb_ref[...] = w_ref[0].astype(BF16)

    @pl.when(blk < nu_ref[0])
    def _():
        o_ref[...] = jnp.dot(a_ref[...], wb_ref[...], preferred_element_type=F32) + b_ref[0]

    @pl.when(blk >= nu_ref[0])
    def _():
        o_ref[...] = jnp.zeros_like(o_ref)


def moe_down(a_sorted, w_down, b_down, block_expert, n_used, *, tm, tn):
    r, d_ff = a_sorted.shape
    ne, _, d = w_down.shape
    nb, nj = r // tm, d // tn
    return pl.pallas_call(
        _moe_down_kernel,
        grid_spec=pltpu.PrefetchScalarGridSpec(
            num_scalar_prefetch=2, grid=(nj, nb),
            in_specs=[pl.BlockSpec((tm, d_ff), lambda j, i, be, nu: (i, 0)),
                      pl.BlockSpec((1, d_ff, tn), lambda j, i, be, nu: (be[i], 0, j)),
                      pl.BlockSpec((1, 1, tn), lambda j, i, be, nu: (be[i], 0, j))],
            out_specs=pl.BlockSpec((tm, tn), lambda j, i, be, nu: (i, j)),
            scratch_shapes=[pltpu.VMEM((d_ff, tn), BF16)]),
        out_shape=jax.ShapeDtypeStruct((r, d), F32),
        compiler_params=_cparams("arbitrary", "arbitrary"),
        name="moe_down",
    )(block_expert, n_used, a_sorted, w_down, b_down.reshape(ne, 1, d))


def _combine_kernel(slot_ref, x_ref, gate_ref, g_ref, src_ref, o_ref, buf_ref, sem, *, tc, final_norm):
    def row_copy(slot, k, r):
        return pltpu.make_async_copy(src_ref.at[pl.ds(slot, 1), :], buf_ref.at[k, pl.ds(r, 1), :], sem)

    def issue(r, c):
        for k in range(TOP_K):
            row_copy(slot_ref[0, 0, r * TOP_K + k], k, r).start()
        return c

    def wait(r, c):
        for k in range(TOP_K):
            row_copy(0, k, r).wait()
        return c

    lax.fori_loop(0, tc, issue, 0)
    lax.fori_loop(0, tc, wait, 0)
    y = x_ref[...]
    gate = gate_ref[...]
    for k in range(TOP_K):
        y = y + gate[:, k:k + 1] * buf_ref[k]
    o_ref[...] = _rms(y, g_ref[...]) if final_norm else y


def moe_combine(x, gates, slots, out_sorted, g_final, *, tc, final_norm):
    n, d = x.shape
    nt = n // tc
    row = lambda c: pl.BlockSpec((tc, c), lambda i: (i, 0))
    return pl.pallas_call(
        functools.partial(_combine_kernel, tc=tc, final_norm=final_norm),
        grid=(nt,),
        in_specs=[pl.BlockSpec((1, 1, tc * TOP_K), lambda i: (i, 0, 0), memory_space=pltpu.SMEM),
                  row(d), row(LANES), pl.BlockSpec((1, d), lambda i: (0, 0)),
                  pl.BlockSpec(memory_space=pl.ANY)],
        out_specs=row(d),
        out_shape=jax.ShapeDtypeStruct((n, d), F32),
        scratch_shapes=[pltpu.VMEM((TOP_K, tc, d), F32), pltpu.SemaphoreType.DMA],
        compiler_params=_cparams("arbitrary"),
        name="moe_combine",
    )(slots.reshape(nt, 1, tc * TOP_K), x, gates, g_final.reshape(1, d), out_sorted)


def _routing_tables(top_e, n_tokens, tm):
    expert = top_e.reshape(-1)
    n_assign = expert.shape[0]
    nb = -(-n_assign // tm) + N_EXPERTS
    onehot = (expert[:, None] == jnp.arange(N_EXPERTS)[None, :]).astype(jnp.int32)
    csum = jnp.cumsum(onehot, axis=0)
    counts = csum[-1]
    rank = jnp.sum((csum - 1) * onehot, axis=1)
    padded = (counts + tm - 1) // tm * tm
    pad_end = jnp.cumsum(padded)
    pad_start = pad_end - padded
    slot = (pad_start[expert] + rank).astype(jnp.int32)
    token = (jnp.arange(n_assign) // TOP_K).astype(jnp.int32)
    row_token = jnp.zeros((nb * tm,), jnp.int32).at[slot].set(token)
    block_expert = jnp.minimum(jnp.searchsorted(pad_end, jnp.arange(nb) * tm, side='right'),
                               N_EXPERTS - 1).astype(jnp.int32)
    n_used = (pad_end[-1] // tm).astype(jnp.int32).reshape(1)
    return slot.reshape(n_tokens, TOP_K), row_token, block_expert, n_used


HALF_PAGE = PAGE_SIZE // 2


def _page_scores(k_page, q_scaled, bias_cols, w_ones):
    p, h, _ = k_page.shape
    kq = (k_page * q_scaled[None]).reshape(p * h, HEAD_DIM).astype(BF16)
    lhs = jnp.concatenate([kq, bias_cols.reshape(p * h, LANES).astype(BF16)], axis=1)
    s = jnp.dot(lhs, w_ones, preferred_element_type=F32)
    return s.reshape(p, h, w_ones.shape[1])


def _softmax_step(s, v_page, m_ref, l_ref, acc_ref, idx):
    m_old = m_ref[idx]
    m_new = jnp.maximum(m_old, jnp.max(s, axis=0))
    alpha = jnp.exp(m_old - m_new)
    p = jnp.exp(s - m_new[None])
    l_ref[idx] = alpha * l_ref[idx] + jnp.sum(p, axis=0)
    acc_ref[idx] = alpha * acc_ref[idx] + jnp.sum(p * v_page, axis=0)
    m_ref[idx] = m_new


def _fox_decode_kernel(pt_ref, q_ref, kn_ref, vn_ref, aux_ref, sel_ref, *rest, pp):
    k_refs, v_refs = rest[:pp], rest[pp:2 * pp]
    o_ref, m_ref, l_ref, acc_ref = rest[2 * pp:]
    g = pl.program_id(1)
    qs = q_ref[0] * (HEAD_DIM ** -0.5)

    @pl.when(g == 0)
    def _():
        s_self = jnp.sum(qs * kn_ref[0], axis=-1, keepdims=True)
        m_ref[0] = jnp.broadcast_to(s_self, m_ref.shape[1:])
        l_ref[0] = jnp.ones(l_ref.shape[1:], F32)
        acc_ref[0] = vn_ref[0]

    w_ones = jnp.ones((2 * LANES, LANES), BF16)
    sel = sel_ref[...]
    for i in range(pp):
        bias = jnp.concatenate([aux_ref[0, 2 * i + j][None] * sel for j in range(2)], axis=0)
        s = _page_scores(k_refs[i][0], qs, bias, w_ones)
        _softmax_step(s, v_refs[i][0], m_ref, l_ref, acc_ref, 0)

    @pl.when(g == pl.num_programs(1) - 1)
    def _():
        o_ref[0] = acc_ref[0] / l_ref[0]


def _page_specs(pp, n_pages, n_heads, base):
    return [pl.BlockSpec((1, PAGE_SIZE, n_heads, HEAD_DIM), functools.partial(
        lambda b, g, pt, i: (base + pt[b * n_pages + g * pp + i], 0, 0, 0), i=i)) for i in range(pp)]


def _lane_pick_table(n_heads):
    lane = np.arange(LANES)[None, :]
    k = np.arange(HALF_PAGE)[:, None]
    t = ((lane == k) | (lane == HALF_PAGE + k)).astype(np.float32)
    return jnp.asarray(np.broadcast_to(t[:, None, :], (HALF_PAGE, n_heads, LANES)))


def fox_decode(q, k_new, v_new, aux, k_cache, v_cache, page_table, base, *, pp=4):
    b, n_heads, _ = q.shape
    n_pages = page_table.shape[1]
    tok = pl.BlockSpec((1, n_heads, HEAD_DIM), lambda bi, g, pt: (bi, 0, 0))
    pages = _page_specs(pp, n_pages, n_heads, base)
    state = pltpu.VMEM((1, n_heads, LANES), F32)
    return pl.pallas_call(
        functools.partial(_fox_decode_kernel, pp=pp),
        grid_spec=pltpu.PrefetchScalarGridSpec(
            num_scalar_prefetch=1, grid=(b, n_pages // pp),
            in_specs=[tok, tok, tok,
                      pl.BlockSpec((1, 2 * pp, n_heads, LANES), lambda bi, g, pt: (bi, g, 0, 0)),
                      pl.BlockSpec((HALF_PAGE, n_heads, LANES), lambda bi, g, pt: (0, 0, 0))]
                     + pages + pages,
            out_specs=tok,
            scratch_shapes=[state, state, state]),
        out_shape=jax.ShapeDtypeStruct((b, n_heads, HEAD_DIM), F32),
        compiler_params=_cparams("parallel", "arbitrary"),
        name="fox_decode",
    )(page_table.reshape(-1), q, k_new, v_new, aux, _lane_pick_table(n_heads),
      *([k_cache] * pp), *([v_cache] * pp))


def _diff_decode_kernel(pt_ref, q_ref, kn_ref, vn_ref, lq1_ref, lk1_ref, lq2_ref, lk2_ref, gsub_ref,
                        c0_ref, tab_ref, *rest, pp, lam_init):
    k_refs, v_refs = rest[:pp], rest[pp:2 * pp]
    o_ref, m_ref, l_ref, acc_ref = rest[2 * pp:]
    g = pl.program_id(1)
    n_steps = pl.num_programs(1)
    qs = q_ref[0] * (DIFF_QK_DIM ** -0.5)
    first = lax.broadcasted_iota(jnp.int32, qs.shape, 1) < DIFF_QK_DIM

    @pl.when(g == 0)
    def _():
        prod = qs * kn_ref[0]
        for t, half in enumerate((jnp.where(first, prod, 0.0), jnp.where(first, 0.0, prod))):
            m_ref[t] = jnp.broadcast_to(jnp.sum(half, axis=-1, keepdims=True), m_ref.shape[1:])
            l_ref[t] = jnp.ones(l_ref.shape[1:], F32)
            acc_ref[t] = vn_ref[0]

    r = lax.broadcasted_iota(jnp.int32, (2 * LANES, 2 * LANES), 0)
    c = lax.broadcasted_iota(jnp.int32, (2 * LANES, 2 * LANES), 1)
    w_ones = jnp.where((r >= HEAD_DIM) | ((r < DIFF_QK_DIM) == (c < LANES)), 1.0, 0.0).astype(BF16)
    for i in range(pp):
        pages_after = (n_steps * pp - 1 - (g * pp + i)).astype(F32)
        bias = tab_ref[...] + (c0_ref[...] * pages_after)[None]
        s = _page_scores(k_refs[i][0], qs, bias, w_ones)
        for t in range(2):
            _softmax_step(s[:, :, t * LANES:(t + 1) * LANES], v_refs[i][0], m_ref, l_ref, acc_ref, t)

    @pl.when(g == n_steps - 1)
    def _():
        lam = _diff_lambda(lq1_ref, lk1_ref, lq2_ref, lk2_ref, lam_init)
        o_ref[0] = _diff_finish(acc_ref[0] / l_ref[0], acc_ref[1] / l_ref[1], lam, gsub_ref[...], lam_init)


def _alibi_tables(n_heads):
    slopes = np.asarray([2.0 ** (-8.0 * (h + 1) / n_heads) for h in range(n_heads)], np.float32)
    c0 = np.zeros((n_heads, LANES), np.float32)
    c0[:, 0] = -slopes * PAGE_SIZE
    tab = np.zeros((PAGE_SIZE, n_heads, LANES), np.float32)
    tab[:, :, 1] = -slopes[None, :] * (PAGE_SIZE - np.arange(PAGE_SIZE, dtype=np.float32))[:, None]
    for a in (c0 * 255.0, tab):
        assert np.array_equal(a.astype(BF16).astype(np.float32), a)
    return jnp.asarray(c0), jnp.asarray(tab)


def diff_decode(q, k_new, v_new, lams, gsub, lam_init, k_cache, v_cache, page_table, base, *, pp=4):
    b, n_heads, _ = q.shape
    n_pages = page_table.shape[1]
    assert n_pages < 256
    tok = pl.BlockSpec((1, n_heads, HEAD_DIM), lambda bi, g, pt: (bi, 0, 0))
    small = lambda n: pl.BlockSpec((1, n), lambda bi, g, pt: (0, 0))
    pages = _page_specs(pp, n_pages, n_heads, base)
    state = pltpu.VMEM((2, n_heads, LANES), F32)
    return pl.pallas_call(
        functools.partial(_diff_decode_kernel, pp=pp, lam_init=lam_init),
        grid_spec=pltpu.PrefetchScalarGridSpec(
            num_scalar_prefetch=1, grid=(b, n_pages // pp),
            in_specs=[tok, tok, tok, small(DIFF_QK_DIM), small(DIFF_QK_DIM), small(DIFF_QK_DIM),
                      small(DIFF_QK_DIM), small(HEAD_DIM),
                      pl.BlockSpec((n_heads, LANES), lambda bi, g, pt: (0, 0)),
                      pl.BlockSpec((PAGE_SIZE, n_heads, LANES), lambda bi, g, pt: (0, 0, 0))] + pages + pages,
            out_specs=tok,
            scratch_shapes=[state, state, state]),
        out_shape=jax.ShapeDtypeStruct((b, n_heads, HEAD_DIM), F32),
        compiler_params=_cparams("parallel", "arbitrary"),
        name="diff_decode",
    )(page_table.reshape(-1), q, k_new, v_new, *lams, gsub, *_alibi_tables(n_heads),
      *([k_cache] * pp), *([v_cache] * pp))


def _decay_terms(cache_logf, page_table, slogf, base):
    b, n_pages = page_table.shape
    h = cache_logf.shape[-1]
    past = n_pages * PAGE_SIZE
    lf = cache_logf[base + page_table].reshape(b, past, h).astype(F32)
    dcum = jnp.cumsum(jnp.concatenate([lf, slogf], axis=1), axis=1)
    rel = dcum[:, past:past + 1] - dcum[:, :past]
    hi = rel.astype(BF16).astype(F32)
    mid = (rel - hi).astype(BF16).astype(F32)
    both = jnp.stack([hi, mid], axis=1).reshape(b, 2, 2 * n_pages, HALF_PAGE, h)
    return both.transpose(0, 2, 4, 1, 3).reshape(b, 2 * n_pages, h, LANES)


def _pad_lanes(a):
    return jnp.pad(a, [(0, 0)] * (a.ndim - 1) + [(0, LANES - a.shape[-1])])


def kernel(x_prompt, x_sample, cache_fox_k, cache_fox_v, cache_fox_logf, cache_diff_k, cache_diff_v, cache_mem_k, cache_mem_v, page_table, mem_prompt, g_mix, w_in, b_forget, lambda_q1, lambda_k1, lambda_q2, lambda_k2, g_diff_sub, w_out, g_xattn, g_mem, w_mem_kv, w_xq, w_xo, g_ffn, w_router, b_router, w_gate_up, b_gate_up, w_down, b_down, g_final):
    bsz, seq, d = x_prompt.shape
    dbs, dec_seq, _ = x_sample.shape
    assert dec_seq == 1
    depth = g_mix.shape[0]
    n_pool = cache_fox_k.shape[1]
    hf = b_forget.shape[1]
    fox_w = hf * HEAD_DIM
    diff_w = (w_in.shape[2] - 3 * fox_w - hf) // 3
    hd = diff_w // HEAD_DIM
    assert fox_w == diff_w
    n_mem = mem_prompt.shape[1]
    xw = w_xq.shape[2]
    n_p, n_s = bsz * seq, dbs * dec_seq
    tm_p = min(512, n_p)
    tq = min(256, seq)

    fox_kc = cache_fox_k.reshape(depth * n_pool, PAGE_SIZE, hf, HEAD_DIM)
    fox_vc = cache_fox_v.reshape(depth * n_pool, PAGE_SIZE, hf, HEAD_DIM)
    diff_kc = cache_diff_k.reshape(depth * n_pool, PAGE_SIZE, hd, HEAD_DIM)
    diff_vc = cache_diff_v.reshape(depth * n_pool, PAGE_SIZE, hd, HEAD_DIM)
    logf_c = cache_fox_logf.reshape(depth * n_pool, PAGE_SIZE, hf)
    n_exp = w_gate_up.shape[1]
    wgu_all = w_gate_up.reshape(depth * n_exp, d, w_gate_up.shape[3])
    bgu_all = b_gate_up.reshape(depth * n_exp, -1)
    wd_all = w_down.reshape(depth * n_exp, w_down.shape[2], d)
    bd_all = b_down.reshape(depth * n_exp, d)

    xp = x_prompt.reshape(n_p, d)
    xs = x_sample.reshape(n_s, d)
    outs = [[] for _ in range(12)]
    for l in range(depth):
        lam_init = 0.8 - 0.6 * math.exp(-0.3 * l)
        lams = tuple(a[l].reshape(1, DIFF_QK_DIM) for a in (lambda_q1, lambda_k1, lambda_q2, lambda_k2))
        gsub = g_diff_sub[l].reshape(1, HEAD_DIM)
        w_main = jnp.concatenate([w_in[l][:, :3 * fox_w], w_in[l][:, 3 * fox_w + hf:]], axis=1)
        w_fg = _pad_lanes(w_in[l][:, 3 * fox_w:3 * fox_w + hf])
        b_fg = _pad_lanes(b_forget[l].reshape(1, hf))
        w_r = _pad_lanes(w_router[l])
        b_r = _pad_lanes(b_router[l].reshape(1, n_exp))
        last = l == depth - 1

        fq, fk, fv, dq, dk, dv, fg = norm_project(
            xp, g_mix[l], w_main.astype(BF16), [BF16, F32, F32, BF16, F32, F32], fox_w, w_fg.astype(BF16),
            tm=tm_p, tn=512)
        logf, dcum = logf_cumsum(fg.reshape(bsz, seq, LANES), b_fg, True)
        as3 = lambda a: a.reshape(bsz, seq, -1)
        fox_o = fox_prompt(as3(fq), as3(fk), as3(fv), dcum, tq=tq)
        diff_o = diff_prompt(as3(dq), as3(dk), as3(dv), lams, gsub, lam_init, tq=tq)
        xp = out_project(xp, fox_o.reshape(n_p, fox_w), diff_o.reshape(n_p, diff_w), w_out[l].astype(BF16),
                         tm=tm_p, tn=d)
        mk, mv = norm_project(mem_prompt.reshape(bsz * n_mem, d), g_mem[l], w_mem_kv[l].astype(BF16),
                              [F32, F32], xw, tm=min(512, bsz * n_mem), tn=xw)
        xp = cross_attend(xp.reshape(bsz, seq, d), g_xattn[l], w_xq[l].astype(BF16), mk.reshape(bsz, n_mem, xw),
                          mv.reshape(bsz, n_mem, xw), w_xo[l].astype(BF16), tm=min(512, seq)).reshape(n_p, d)
        h_p, e_p, gate_p = route(xp, g_ffn[l], w_r.astype(BF16), b_r, tm=tm_p)

        sq, sk, sv, sdq, sdk, sdv, sfg = norm_project(
            xs, g_mix[l], w_main, [F32] * 6, fox_w, w_fg, tm=n_s, tn=512, precise=True)
        slogf, = logf_cumsum(sfg.reshape(dbs, 1, LANES), b_fg, False)
        tok3 = lambda a: a.reshape(dbs, -1, HEAD_DIM)
        aux = _decay_terms(logf_c, page_table, slogf[:, :, :hf], l * n_pool)
        fox_os = fox_decode(tok3(sq), tok3(sk), tok3(sv), aux, fox_kc, fox_vc, page_table, l * n_pool)
        diff_os = diff_decode(tok3(sdq), tok3(sdk), tok3(sdv), lams, gsub, lam_init, diff_kc, diff_vc,
                              page_table, l * n_pool)
        xs = out_project(xs, fox_os.reshape(n_s, fox_w), diff_os.reshape(n_s, diff_w), w_out[l],
                         tm=n_s, tn=512, precise=True)
        xs = cross_attend(xs.reshape(dbs, 1, d), g_xattn[l], w_xq[l], cache_mem_k[l].reshape(dbs, n_mem, xw),
                          cache_mem_v[l].reshape(dbs, n_mem, xw), w_xo[l], tm=1, precise=True).reshape(n_s, d)
        h_s, e_s, gate_s = route(xs, g_ffn[l], w_r, b_r, tm=n_s, precise=True)

        n_t = n_p + n_s
        h_all = jnp.concatenate([h_p, h_s], axis=0)
        e_all = jnp.concatenate([e_p[:, :TOP_K], e_s[:, :TOP_K]], axis=0)
        slots, row_token, block_expert, n_used = _routing_tables(e_all, n_t, MOE_TM)
        block_expert = block_expert + l * n_exp
        x_sorted = moe_gather(h_all, row_token, n_used, tm=MOE_TM)
        a_sorted = moe_up(x_sorted, wgu_all, bgu_all, block_expert, n_used, tm=MOE_TM, tn=512)
        o_sorted = moe_down(a_sorted, wd_all, bd_all, block_expert, n_used, tm=MOE_TM, tn=512)
        xp = moe_combine(xp, gate_p, slots[:n_p], o_sorted, g_final, tc=min(128, n_p), final_norm=last)
        xs = moe_combine(xs, gate_s, slots[n_p:], o_sorted, g_final, tc=n_s, final_norm=last)

        new = (fk.reshape(bsz, seq, hf, HEAD_DIM), fv.reshape(bsz, seq, hf, HEAD_DIM), logf[:, :, :hf],
               dk.reshape(bsz, seq, hd, HEAD_DIM), dv.reshape(bsz, seq, hd, HEAD_DIM),
               mk.reshape(bsz, n_mem, N_XATTN_HEADS, xw // N_XATTN_HEADS),
               mv.reshape(bsz, n_mem, N_XATTN_HEADS, xw // N_XATTN_HEADS),
               sk.reshape(dbs, 1, hf, HEAD_DIM), sv.reshape(dbs, 1, hf, HEAD_DIM), slogf[:, :, :hf],
               sdk.reshape(dbs, 1, hd, HEAD_DIM), sdv.reshape(dbs, 1, hd, HEAD_DIM))
        for acc, a in zip(outs, new):
            acc.append(a)

    return (xp.reshape(bsz, seq, d), xs.reshape(dbs, dec_seq, d)) + tuple(jnp.stack(a) for a in outs)
```

```python
import functools
import math

import jax
import jax.numpy as jnp
import numpy as np
from jax import lax
from jax.experimental import pallas as pl
from jax.experimental.pallas import tpu as pltpu

F32 = jnp.float32
BF16 = jnp.bfloat16
U32 = jnp.uint32
HIGHEST = lax.Precision.HIGHEST

HEAD_DIM = 128
DIFF_QK_DIM = 64
PAGE_SIZE = 128
HALF_PAGE = PAGE_SIZE // 2
N_XATTN_HEADS = 4
N_EXPERTS = 32
TOP_K = 4
SWIGLU_LIMIT = 7.0
SWIGLU_ALPHA = 1.702
NORM_EPS = 1e-5
LANES = 128
VMEM_LIMIT = 56 * 1024 * 1024
NEG_BIG = -1e30
LOG2E = math.log2(math.e)
MOE_TM = 512
PAGE_CHUNK = 4


def _cparams(*sem):
    return pltpu.CompilerParams(dimension_semantics=sem, vmem_limit_bytes=VMEM_LIMIT)


def _rms(x, g):
    return x * lax.rsqrt(jnp.mean(x * x, axis=-1, keepdims=True) + NORM_EPS) * g


def _dot(a, b):
    return jnp.dot(a.astype(BF16), b.astype(BF16), preferred_element_type=F32)


def _dot_nt(a, b):
    dn = (((1,), (1,)), ((), ()))
    return lax.dot_general(a.astype(BF16), b.astype(BF16), dn, preferred_element_type=F32)


def _proj_kernel(*refs, group_tiles, has_scale, has_extra):
    x_ref, g_ref, w_ref = refs[:3]
    pos = 3
    sc_ref = wx_ref = ox_ref = None
    if has_scale:
        sc_ref = refs[pos]
        pos += 1
    if has_extra:
        wx_ref = refs[pos]
        pos += 1
    outs = refs[pos:pos + len(group_tiles)]
    pos += len(group_tiles)
    if has_extra:
        ox_ref = refs[pos]
        pos += 1
    h_ref = refs[pos]
    j = pl.program_id(1)

    @pl.when(j == 0)
    def _():
        h = _rms(x_ref[...], g_ref[...])
        h_ref[...] = h.astype(h_ref.dtype)
        if has_extra:
            ox_ref[...] = _dot(h_ref[...], wx_ref[...])

    acc = _dot(h_ref[...], w_ref[...])
    if has_scale:
        acc = acc * sc_ref[...]
    lo = 0
    for k, nt in enumerate(group_tiles):
        @pl.when((j >= lo) & (j < lo + nt))
        def _(k=k):
            outs[k][...] = acc.astype(outs[k].dtype)
        lo += nt


def norm_project(x, g, w, groups, col_scale=None, w_extra=None, *, tm, tn):
    n, d = x.shape
    assert w.shape[1] == sum(wd for _, wd in groups) and n % tm == 0
    group_tiles = tuple(wd // tn for _, wd in groups)
    assert all(wd % tn == 0 for _, wd in groups)
    has_scale, has_extra = col_scale is not None, w_extra is not None
    in_specs = [pl.BlockSpec((tm, d), lambda i, j: (i, 0)),
                pl.BlockSpec((1, d), lambda i, j: (0, 0)),
                pl.BlockSpec((d, tn), lambda i, j: (0, j))]
    args = [x, g.reshape(1, d), w]
    if has_scale:
        in_specs.append(pl.BlockSpec((1, tn), lambda i, j: (0, j)))
        args.append(col_scale)
    if has_extra:
        in_specs.append(pl.BlockSpec((d, LANES), lambda i, j: (0, 0)))
        args.append(w_extra)
    out_specs, out_shape, lo = [], [], 0
    for (dt, wd), nt in zip(groups, group_tiles):
        out_specs.append(pl.BlockSpec((tm, tn), functools.partial(
            lambda i, j, lo, nt: (i, jnp.clip(j - lo, 0, nt - 1)), lo=lo, nt=nt)))
        out_shape.append(jax.ShapeDtypeStruct((n, wd), dt))
        lo += nt
    if has_extra:
        out_specs.append(pl.BlockSpec((tm, LANES), lambda i, j: (i, 0)))
        out_shape.append(jax.ShapeDtypeStruct((n, LANES), F32))
    return pl.pallas_call(
        functools.partial(_proj_kernel, group_tiles=group_tiles, has_scale=has_scale, has_extra=has_extra),
        grid=(n // tm, sum(group_tiles)),
        in_specs=in_specs, out_specs=out_specs, out_shape=out_shape,
        scratch_shapes=[pltpu.VMEM((tm, d), BF16)],
        compiler_params=_cparams("parallel", "arbitrary"),
        name="norm_project",
    )(*args)


def _logf_kernel(fg_ref, b_ref, logf_ref, *rest, seq, blk, do_cumsum):
    x = fg_ref[0] + b_ref[...]
    lf = jnp.minimum(x, 0.0) - jnp.log1p(jnp.exp(-jnp.abs(x)))
    logf_ref[0] = lf
    if do_cumsum:
        dcum_ref = rest[0]
        r = lax.broadcasted_iota(jnp.int32, (blk, blk), 0)
        c = lax.broadcasted_iota(jnp.int32, (blk, blk), 1)
        tri = (c <= r).astype(F32)
        carry = jnp.zeros((1, LANES), F32)
        for i in range(seq // blk):
            part = jnp.dot(tri, lf[i * blk:(i + 1) * blk], preferred_element_type=F32, precision=HIGHEST) + carry
            dcum_ref[0, i * blk:(i + 1) * blk, :] = part
            carry = part[blk - 1:blk, :]


def logf_cumsum(fg, b_pad, do_cumsum):
    b, s, _ = fg.shape
    blk = min(s, 256)
    spec = pl.BlockSpec((1, s, LANES), lambda i: (i, 0, 0))
    n_out = 2 if do_cumsum else 1
    return pl.pallas_call(
        functools.partial(_logf_kernel, seq=s, blk=blk, do_cumsum=do_cumsum),
        grid=(b,),
        in_specs=[spec, pl.BlockSpec((1, LANES), lambda i: (0, 0))],
        out_specs=[spec] * n_out,
        out_shape=[jax.ShapeDtypeStruct((b, s, LANES), F32)] * n_out,
        compiler_params=_cparams("parallel"),
        name="logf_cumsum",
    )(fg, b_pad)


def _stage_kv(k_ref, v_ref, kb_ref, vb_ref):
    kb_ref[...] = k_ref[0].astype(BF16)
    vb_ref[:, :HEAD_DIM] = v_ref[0].astype(BF16)
    vb_ref[:, HEAD_DIM:] = jnp.ones((vb_ref.shape[0], HEAD_DIM), BF16)


def _flash_update(s, vblk, m, acc):
    m_new = jnp.maximum(m, jnp.max(s, axis=-1, keepdims=True))
    alpha = jnp.exp2(m - m_new)
    p = jnp.exp2(s - m_new).astype(BF16)
    return m_new, alpha * acc + jnp.dot(p, vblk, preferred_element_type=F32)


def _flash_init(tq):
    return jnp.full((tq, 1), NEG_BIG, F32), jnp.zeros((tq, 2 * HEAD_DIM), F32)


def _flash_out(acc):
    return acc[:, :HEAD_DIM] / acc[:, HEAD_DIM:HEAD_DIM + 1]


def _causal_mask(t):
    r = lax.broadcasted_iota(jnp.int32, (t, t), 0)
    c = lax.broadcasted_iota(jnp.int32, (t, t), 1)
    return c <= r


def _fox_prompt_kernel(q_ref, k_ref, v_ref, drow_ref, o_ref, kb_ref, vb_ref, *, t):
    qi = pl.program_id(2)

    @pl.when(qi == 0)
    def _():
        _stage_kv(k_ref, v_ref, kb_ref, vb_ref)

    q = q_ref[0]
    d_first = drow_ref[0, 0, qi][:, :1]

    def step(kb, carry, masked):
        start = pl.multiple_of(kb * t, t)
        s = _dot_nt(q, kb_ref[pl.ds(start, t), :]) + (d_first - drow_ref[0, 0, kb]) * LOG2E
        if masked:
            s = jnp.where(_causal_mask(t), s, NEG_BIG)
        return _flash_update(s, vb_ref[pl.ds(start, t), :], *carry)

    carry = lax.fori_loop(0, qi, lambda kb, cr: step(kb, cr, False), _flash_init(t))
    _, acc = step(qi, carry, True)
    o_ref[0] = _flash_out(acc).astype(o_ref.dtype)


def _attn_specs(s, t, q_off, k_off, v_off):
    return [pl.BlockSpec((1, t, HEAD_DIM), lambda bi, hi, qi: (bi, qi, q_off + hi)),
            pl.BlockSpec((1, s, HEAD_DIM), lambda bi, hi, qi: (bi, 0, k_off + hi)),
            pl.BlockSpec((1, s, HEAD_DIM), lambda bi, hi, qi: (bi, 0, v_off + hi))]


def fox_prompt(qa, kv, dcum, n_heads, q_off, k_off, v_off, *, t):
    b, s, _ = qa.shape
    nq = s // t
    drow = jnp.swapaxes(dcum[:, :, :n_heads], 1, 2).reshape(b, n_heads, nq, 1, t)
    return pl.pallas_call(
        functools.partial(_fox_prompt_kernel, t=t),
        grid=(b, n_heads, nq),
        in_specs=_attn_specs(s, t, q_off, k_off, v_off)
                 + [pl.BlockSpec((1, 1, nq, 1, t), lambda bi, hi, qi: (bi, hi, 0, 0, 0))],
        out_specs=pl.BlockSpec((1, t, HEAD_DIM), lambda bi, hi, qi: (bi, qi, hi)),
        out_shape=jax.ShapeDtypeStruct((b, s, n_heads * HEAD_DIM), BF16),
        scratch_shapes=[pltpu.VMEM((s, HEAD_DIM), BF16), pltpu.VMEM((s, 2 * HEAD_DIM), BF16)],
        compiler_params=_cparams("parallel", "parallel", "arbitrary"),
        name="fox_prompt",
    )(qa, kv, kv, drow)


def _diff_lambda(lq1_ref, lk1_ref, lq2_ref, lk2_ref, lam_init):
    a = jnp.exp(jnp.sum(lq1_ref[...] * lk1_ref[...], axis=-1, keepdims=True))
    b = jnp.exp(jnp.sum(lq2_ref[...] * lk2_ref[...], axis=-1, keepdims=True))
    return a - b + lam_init


def _diff_finish(o1, o2, lam, gsub, lam_init):
    o = o1 - lam * o2
    return _rms(o, gsub) * (1.0 - lam_init)


def _diff_prompt_kernel(q_ref, k_ref, v_ref, slope_ref, lq1_ref, lk1_ref, lq2_ref, lk2_ref, gsub_ref,
                        o_ref, kb_ref, vb_ref, *, t, lam_init):
    qi = pl.program_id(2)

    @pl.when(qi == 0)
    def _():
        _stage_kv(k_ref, v_ref, kb_ref, vb_ref)

    q = q_ref[0]
    first = lax.broadcasted_iota(jnp.int32, (t, HEAD_DIM), 1) < DIFF_QK_DIM
    halves = (jnp.where(first, q, jnp.zeros_like(q)), jnp.where(first, jnp.zeros_like(q), q))
    slope2 = slope_ref[0, :, :1] * LOG2E
    col = lax.broadcasted_iota(jnp.int32, (1, t), 1).astype(F32)

    def step(kb, carry, masked):
        start = pl.multiple_of(kb * t, t)
        kblk = kb_ref[pl.ds(start, t), :]
        vblk = vb_ref[pl.ds(start, t), :]
        bias = slope2 * (col + ((kb - qi) * t).astype(F32))
        out = []
        for qh, cr in zip(halves, carry):
            s = _dot_nt(qh, kblk) + bias
            if masked:
                s = jnp.where(_causal_mask(t), s, NEG_BIG)
            out.append(_flash_update(s, vblk, *cr))
        return tuple(out)

    carry = lax.fori_loop(0, qi, lambda kb, cr: step(kb, cr, False), (_flash_init(t), _flash_init(t)))
    (_, a1), (_, a2) = step(qi, carry, True)
    lam = _diff_lambda(lq1_ref, lk1_ref, lq2_ref, lk2_ref, lam_init)
    o_ref[0] = _diff_finish(_flash_out(a1), _flash_out(a2), lam, gsub_ref[...], lam_init).astype(o_ref.dtype)


def _alibi_slopes(n_heads):
    s = jnp.asarray([2.0 ** (-8.0 * (h + 1) / n_heads) for h in range(n_heads)], F32)
    return jnp.broadcast_to(s[:, None, None], (n_heads, 1, LANES))


def diff_prompt(qa, kv, lams, gsub, lam_init, n_heads, q_off, k_off, v_off, *, t):
    b, s, _ = qa.shape
    small = lambda n: pl.BlockSpec((1, n), lambda bi, hi, qi: (0, 0))
    return pl.pallas_call(
        functools.partial(_diff_prompt_kernel, t=t, lam_init=lam_init),
        grid=(b, n_heads, s // t),
        in_specs=_attn_specs(s, t, q_off, k_off, v_off)
                 + [pl.BlockSpec((1, 1, LANES), lambda bi, hi, qi: (hi, 0, 0)),
                    small(DIFF_QK_DIM), small(DIFF_QK_DIM), small(DIFF_QK_DIM), small(DIFF_QK_DIM),
                    small(HEAD_DIM)],
        out_specs=pl.BlockSpec((1, t, HEAD_DIM), lambda bi, hi, qi: (bi, qi, hi)),
        out_shape=jax.ShapeDtypeStruct((b, s, n_heads * HEAD_DIM), BF16),
        scratch_shapes=[pltpu.VMEM((s, HEAD_DIM), BF16), pltpu.VMEM((s, 2 * HEAD_DIM), BF16)],
        compiler_params=_cparams("parallel", "parallel", "arbitrary"),
        name="diff_prompt",
    )(qa, kv, kv, _alibi_slopes(n_heads), *lams, gsub)


def _outproj_kernel(x_ref, a1_ref, a2_ref, w1_ref, w2_ref, o_ref):
    o_ref[...] = (x_ref[...] + _dot(a1_ref[...], w1_ref[...])
                  + _dot(a2_ref[...], w2_ref[...]))


def out_project(x, a1, a2, w, *, tm, tn):
    n, d = x.shape
    half = a1.shape[1]
    assert w.shape == (2 * half, d) and half % 8 == 0
    return pl.pallas_call(
        _outproj_kernel,
        grid=(n // tm, d // tn),
        in_specs=[pl.BlockSpec((tm, tn), lambda i, j: (i, j)),
                  pl.BlockSpec((tm, half), lambda i, j: (i, 0)),
                  pl.BlockSpec((tm, half), lambda i, j: (i, 0)),
                  pl.BlockSpec((half, tn), lambda i, j: (0, j)),
                  pl.BlockSpec((half, tn), lambda i, j: (1, j))],
        out_specs=pl.BlockSpec((tm, tn), lambda i, j: (i, j)),
        out_shape=jax.ShapeDtypeStruct((n, d), F32),
        compiler_params=_cparams("parallel", "arbitrary"),
        name="out_project",
    )(x, a1, a2, w, w)


def _xattn_kernel(x_ref, g_ref, wq_ref, mk_ref, mv_ref, wo_ref, o_ref):
    x = x_ref[0]
    h = _rms(x, g_ref[...])
    q = _dot(h, wq_ref[...])
    hd = q.shape[1] // N_XATTN_HEADS
    scale = hd ** -0.5
    outs = []
    for i in range(N_XATTN_HEADS):
        sl = slice(i * hd, (i + 1) * hd)
        s = _dot_nt(q[:, sl], mk_ref[0, :, sl]) * scale
        s = s - jnp.max(s, axis=-1, keepdims=True)
        p = jnp.exp(s)
        p = p / jnp.sum(p, axis=-1, keepdims=True)
        outs.append(_dot(p, mv_ref[0, :, sl]))
    o = jnp.concatenate(outs, axis=-1)
    o_ref[0] = x + _dot(o, wo_ref[...])


def cross_attend(x, g, wq, mk, mv, wo, *, tm):
    b, t, d = x.shape
    m, xw = mk.shape[1:]
    return pl.pallas_call(
        _xattn_kernel,
        grid=(b, t // tm),
        in_specs=[pl.BlockSpec((1, tm, d), lambda bi, i: (bi, i, 0)),
                  pl.BlockSpec((1, d), lambda bi, i: (0, 0)),
                  pl.BlockSpec((d, xw), lambda bi, i: (0, 0)),
                  pl.BlockSpec((1, m, xw), lambda bi, i: (bi, 0, 0)),
                  pl.BlockSpec((1, m, xw), lambda bi, i: (bi, 0, 0)),
                  pl.BlockSpec((xw, d), lambda bi, i: (0, 0))],
        out_specs=pl.BlockSpec((1, tm, d), lambda bi, i: (bi, i, 0)),
        out_shape=jax.ShapeDtypeStruct((b, t, d), F32),
        compiler_params=_cparams("parallel", "arbitrary"),
        name="cross_attend",
    )(x, g.reshape(1, d), wq, mk, mv, wo)


def _router_kernel(x_ref, g_ref, w_ref, b_ref, cin_ref, hp_ref, e_ref, gate_ref, rank_ref, cnt_ref, carry_ref):
    @pl.when(pl.program_id(0) == 0)
    def _():
        carry_ref[...] = cin_ref[...]

    h = _rms(x_ref[...], g_ref[...])
    bits = lax.bitcast_convert_type(h.astype(BF16).astype(F32), U32)
    half = bits.shape[1] // 2
    hp_ref[...] = (bits[:, :half] >> 16) | bits[:, half:]

    logits = _dot(h, w_ref[...]) + b_ref[...]
    tm = logits.shape[0]
    lane = lax.broadcasted_iota(jnp.int32, (tm, LANES), 1)
    logits = jnp.where(lane < N_EXPERTS, logits, -jnp.inf)
    e_out = jnp.zeros((tm, LANES), jnp.int32)
    v_out = jnp.full((tm, LANES), -jnp.inf, F32)
    picks = []
    for k in range(TOP_K):
        mx = jnp.max(logits, axis=-1, keepdims=True)
        idx = jnp.min(jnp.where(logits == mx, lane, LANES), axis=-1, keepdims=True)
        picks.append(lane == idx)
        e_out = jnp.where(lane == k, idx, e_out)
        v_out = jnp.where(lane == k, mx, v_out)
        logits = jnp.where(picks[-1], -jnp.inf, logits)
    top = jnp.max(v_out, axis=-1, keepdims=True)
    p = jnp.exp(v_out - top)
    gate_ref[...] = p / jnp.sum(p, axis=-1, keepdims=True)
    e_ref[...] = e_out

    member = functools.reduce(jnp.logical_or, picks).astype(F32)
    r = lax.broadcasted_iota(jnp.int32, (tm, tm), 0)
    c = lax.broadcasted_iota(jnp.int32, (tm, tm), 1)
    before = jnp.dot((c < r).astype(BF16), member.astype(BF16), preferred_element_type=F32) + carry_ref[...]
    rank = jnp.zeros((tm, LANES), jnp.int32)
    for k in range(TOP_K):
        rk = jnp.sum(jnp.where(picks[k], before, 0.0), axis=-1, keepdims=True)
        rank = jnp.where(lane == k, rk.astype(jnp.int32), rank)
    rank_ref[...] = rank
    carry_ref[...] = carry_ref[...] + jnp.sum(member, axis=0, keepdims=True)
    cnt_ref[...] = carry_ref[...]


def route(x, g, w_pad, b_pad, count_in, *, tm):
    n, d = x.shape
    row = lambda c: pl.BlockSpec((tm, c), lambda i: (i, 0))
    one = lambda c: pl.BlockSpec((1, c), lambda i: (0, 0))
    return pl.pallas_call(
        _router_kernel,
        grid=(n // tm,),
        in_specs=[row(d), one(d), pl.BlockSpec((d, LANES), lambda i: (0, 0)), one(LANES), one(LANES)],
        out_specs=[row(d // 2), row(LANES), row(LANES), row(LANES), one(LANES)],
        out_shape=[jax.ShapeDtypeStruct((n, d // 2), U32), jax.ShapeDtypeStruct((n, LANES), jnp.int32),
                   jax.ShapeDtypeStruct((n, LANES), F32), jax.ShapeDtypeStruct((n, LANES), jnp.int32),
                   jax.ShapeDtypeStruct((1, LANES), F32)],
        scratch_shapes=[pltpu.VMEM((1, LANES), F32)],
        compiler_params=_cparams("arbitrary"),
        name="route",
    )(x, g.reshape(1, d), w_pad, b_pad, count_in)


def _dispatch_kernel(slot_ref, src_ref, dst_in_ref, dst_ref, sem, *, td):
    base = pl.program_id(0) * td

    def row_copy(t, slot):
        return pltpu.make_async_copy(src_ref.at[pl.ds(t, 1), :], dst_ref.at[pl.ds(slot, 1), :], sem)

    def issue(r, c):
        for k in range(TOP_K):
            row_copy(base + r, slot_ref[0, 0, r * TOP_K + k]).start()
        return c

    def wait(r, c):
        for k in range(TOP_K):
            row_copy(0, 0).wait()
        return c

    lax.fori_loop(0, td, issue, 0, unroll=4)
    lax.fori_loop(0, td, wait, 0, unroll=4)


def moe_dispatch(h_packed, slots, x_sorted, *, td):
    n = h_packed.shape[0]
    nt = n // td
    any_spec = pl.BlockSpec(memory_space=pl.ANY)
    return pl.pallas_call(
        functools.partial(_dispatch_kernel, td=td),
        grid=(nt,),
        in_specs=[pl.BlockSpec((1, 1, td * TOP_K), lambda i: (i, 0, 0), memory_space=pltpu.SMEM),
                  any_spec, any_spec],
        out_specs=any_spec,
        out_shape=jax.ShapeDtypeStruct(x_sorted.shape, x_sorted.dtype),
        scratch_shapes=[pltpu.SemaphoreType.DMA],
        input_output_aliases={2: 0},
        compiler_params=_cparams("arbitrary"),
        name="moe_dispatch",
    )(slots.reshape(nt, 1, td * TOP_K), h_packed, x_sorted)


def _unpack_bf16_pairs(words):
    lo = lax.bitcast_convert_type(words << 16, F32)
    hi = lax.bitcast_convert_type(words & jnp.uint32(0xFFFF0000), F32)
    return jnp.concatenate([lo, hi], axis=1).astype(BF16)


def _expert_changed(be_ref, blk):
    return (blk == 0) | (be_ref[blk] != be_ref[jnp.maximum(blk - 1, 0)])


def _moe_up_kernel(be_ref, nu_ref, x_ref, wg_ref, wu_ref, bg_ref, bu_ref, o_ref, wgb_ref, wub_ref):
    blk = pl.program_id(1)

    @pl.when(_expert_changed(be_ref, blk))
    def _():
        wgb_ref[...] = wg_ref[0].astype(BF16)
        wub_ref[...] = wu_ref[0].astype(BF16)

    @pl.when(blk < nu_ref[0])
    def _():
        x = _unpack_bf16_pairs(x_ref[...])
        g = jnp.dot(x, wgb_ref[...], preferred_element_type=F32) + bg_ref[0]
        u = jnp.dot(x, wub_ref[...], preferred_element_type=F32) + bu_ref[0]
        g = jnp.minimum(g, SWIGLU_LIMIT)
        u = jnp.clip(u, -SWIGLU_LIMIT, SWIGLU_LIMIT)
        o_ref[...] = ((u + 1.0) * (g * jax.nn.sigmoid(SWIGLU_ALPHA * g))).astype(o_ref.dtype)

    @pl.when(blk >= nu_ref[0])
    def _():
        o_ref[...] = jnp.zeros_like(o_ref)


def _used_block(i, nu):
    return jnp.minimum(i, nu[0] - 1)


def moe_up(x_sorted, w_gate_up, b_gate_up, block_expert, n_used, *, tm, tn):
    r, dh = x_sorted.shape
    ne, d, two_ff = w_gate_up.shape
    d_ff = two_ff // 2
    nb, nj = r // tm, d_ff // tn
    bias = b_gate_up.reshape(ne, 1, two_ff)
    return pl.pallas_call(
        _moe_up_kernel,
        grid_spec=pltpu.PrefetchScalarGridSpec(
            num_scalar_prefetch=2, grid=(nj, nb),
            in_specs=[pl.BlockSpec((tm, dh), lambda j, i, be, nu: (_used_block(i, nu), 0)),
                      pl.BlockSpec((1, d, tn), lambda j, i, be, nu: (be[i], 0, j)),
                      pl.BlockSpec((1, d, tn), lambda j, i, be, nu: (be[i], 0, nj + j)),
                      pl.BlockSpec((1, 1, tn), lambda j, i, be, nu: (be[i], 0, j)),
                      pl.BlockSpec((1, 1, tn), lambda j, i, be, nu: (be[i], 0, nj + j))],
            out_specs=pl.BlockSpec((tm, tn), lambda j, i, be, nu: (i, j)),
            scratch_shapes=[pltpu.VMEM((d, tn), BF16), pltpu.VMEM((d, tn), BF16)]),
        out_shape=jax.ShapeDtypeStruct((r, d_ff), BF16),
        compiler_params=_cparams("arbitrary", "arbitrary"),
        name="moe_up",
    )(block_expert, n_used, x_sorted, w_gate_up, w_gate_up, bias, bias)


def _moe_down_kernel(be_ref, nu_ref, a_ref, w_ref, b_ref, o_ref, wb_ref):
    blk = pl.program_id(1)

    @pl.when(_expert_changed(be_ref, blk))
    def _():
        wb_ref[...] = w_ref[0].astype(BF16)

    @pl.when(blk < nu_ref[0])
    def _():
        o_ref[...] = jnp.dot(a_ref[...], wb_ref[...], preferred_element_type=F32) + b_ref[0]

    @pl.when(blk >= nu_ref[0])
    def _():
        o_ref[...] = jnp.zeros_like(o_ref)


def moe_down(a_sorted, w_down, b_down, block_expert, n_used, *, tm, tn):
    r, d_ff = a_sorted.shape
    ne, _, d = w_down.shape
    nb, nj = r // tm, d // tn
    return pl.pallas_call(
        _moe_down_kernel,
        grid_spec=pltpu.PrefetchScalarGridSpec(
            num_scalar_prefetch=2, grid=(nj, nb),
            in_specs=[pl.BlockSpec((tm, d_ff), lambda j, i, be, nu: (_used_block(i, nu), 0)),
                      pl.BlockSpec((1, d_ff, tn), lambda j, i, be, nu: (be[i], 0, j)),
                      pl.BlockSpec((1, 1, tn), lambda j, i, be, nu: (be[i], 0, j))],
            out_specs=pl.BlockSpec((tm, tn), lambda j, i, be, nu: (i, j)),
            scratch_shapes=[pltpu.VMEM((d_ff, tn), BF16)]),
        out_shape=jax.ShapeDtypeStruct((r, d), F32),
        compiler_params=_cparams("arbitrary", "arbitrary"),
        name="moe_down",
    )(block_expert, n_used, a_sorted, w_down, b_down.reshape(ne, 1, d))


def _combine_kernel(slot_ref, x_ref, gate_ref, g_ref, src_ref, o_ref, buf_ref, sem, *, tc, final_norm):
    def row_copy(slot, k, r):
        return pltpu.make_async_copy(src_ref.at[pl.ds(slot, 1), :], buf_ref.at[k, pl.ds(r, 1), :], sem)

    def issue(r, c):
        for k in range(TOP_K):
            row_copy(slot_ref[0, 0, r * TOP_K + k], k, r).start()
        return c

    def wait(r, c):
        for k in range(TOP_K):
            row_copy(0, k, r).wait()
        return c

    lax.fori_loop(0, tc, issue, 0, unroll=4)
    lax.fori_loop(0, tc, wait, 0, unroll=4)
    y = x_ref[...]
    gate = gate_ref[...]
    for k in range(TOP_K):
        y = y + gate[:, k:k + 1] * buf_ref[k]
    o_ref[...] = _rms(y, g_ref[...]) if final_norm else y


def moe_combine(x, gates, slots, out_sorted, g_final, *, tc, final_norm):
    n, d = x.shape
    nt = n // tc
    row = lambda c: pl.BlockSpec((tc, c), lambda i: (i, 0))
    return pl.pallas_call(
        functools.partial(_combine_kernel, tc=tc, final_norm=final_norm),
        grid=(nt,),
        in_specs=[pl.BlockSpec((1, 1, tc * TOP_K), lambda i: (i, 0, 0), memory_space=pltpu.SMEM),
                  row(d), row(LANES), pl.BlockSpec((1, d), lambda i: (0, 0)),
                  pl.BlockSpec(memory_space=pl.ANY)],
        out_specs=row(d),
        out_shape=jax.ShapeDtypeStruct((n, d), F32),
        scratch_shapes=[pltpu.VMEM((TOP_K, tc, d), F32), pltpu.SemaphoreType.DMA],
        compiler_params=_cparams("arbitrary"),
        name="moe_combine",
    )(slots.reshape(nt, 1, tc * TOP_K), x, gates, g_final.reshape(1, d), out_sorted)


def _block_tables(counts, n_assign, tm):
    nb = -(-n_assign // tm) + N_EXPERTS
    padded = (counts + tm - 1) // tm * tm
    pad_end = jnp.cumsum(padded)
    first_row = (jnp.arange(nb) * tm)[:, None]
    block_expert = jnp.minimum(jnp.sum(pad_end[None, :] <= first_row, axis=1), N_EXPERTS - 1).astype(jnp.int32)
    n_used = (pad_end[-1] // tm).astype(jnp.int32).reshape(1)
    return pad_end - padded, block_expert, n_used, nb


GROUP = LANES // 8


def _rounded(x):
    return x.astype(BF16).astype(F32)


def _group_rows(q):
    h = q.shape[0]
    return jnp.broadcast_to(q[:, None, :], (h, GROUP, HEAD_DIM)).reshape(h * GROUP, HEAD_DIM)


def _group_mask(n_heads):
    lane = lax.broadcasted_iota(jnp.int32, (n_heads, LANES), 1)
    head = lax.broadcasted_iota(jnp.int32, (n_heads, LANES), 0)
    return lane // GROUP == head


def _own_lanes(x, mask):
    own = jnp.sum(jnp.where(mask, x, 0.0), axis=-1, keepdims=True) * (1.0 / GROUP)
    return jnp.broadcast_to(own, x.shape)


def _page_scores(k_page, bias_rows, wt, c):
    p, h, _ = k_page.shape
    lhs = jnp.concatenate([k_page.reshape(p * h, HEAD_DIM).astype(BF16), bias_rows], axis=1)
    s = lax.dot_general(lhs, wt, (((1,), (1,)), ((), ())), preferred_element_type=F32)
    return (s * c).reshape(p, h, wt.shape[0])


def _stats_update(scores, offsets, m_ref, l_ref, idx):
    m_old = m_ref[idx]
    m_new = functools.reduce(jnp.maximum, [jnp.max(s, axis=0) + off for s, off in zip(scores, offsets)], m_old)
    l = jnp.exp2(m_old - m_new) * l_ref[idx]
    for s, off in zip(scores, offsets):
        l = l + jnp.sum(jnp.exp2(s - (m_new - off)[None]), axis=0)
    m_ref[idx], l_ref[idx] = m_new, l


def _spread_matrix():
    r = lax.broadcasted_iota(jnp.int32, (2 * LANES, 2 * LANES), 0)
    c = lax.broadcasted_iota(jnp.int32, (2 * LANES, 2 * LANES), 1)
    return jnp.where((r < LANES) == (c < LANES), 1.0 / GROUP, 0.0).astype(BF16)


def _weigh_page_pair(p_pair, v_pair, e_spread):
    n, h, _ = p_pair[0].shape
    lhs = jnp.concatenate([p.reshape(n * h, LANES).astype(BF16) for p in p_pair], axis=1)
    pe = jnp.dot(lhs, e_spread, preferred_element_type=F32)
    out = 0.0
    for t, v_page in enumerate(v_pair):
        vb = _rounded(v_page.reshape(n * h, HEAD_DIM))
        out = out + jnp.sum((pe[:, t * LANES:(t + 1) * LANES] * vb).reshape(n, h, HEAD_DIM), axis=0)
    return out


def _fox_decode_kernel(pt_ref, q_ref, kn_ref, vn_ref, aux_ref, sel_ref, *rest, pp):
    k_refs, v_refs = rest[:pp], rest[pp:2 * pp]
    o_ref, wt_ref, m_ref, l_ref, acc_ref = rest[2 * pp:]
    ph, g = pl.program_id(1), pl.program_id(2)
    n_heads = q_ref.shape[1]
    c = HEAD_DIM ** -0.5 * LOG2E
    mask = _group_mask(n_heads)

    def self_score():
        return jnp.sum(q_ref[0] * kn_ref[0], axis=-1, keepdims=True) * c

    @pl.when((ph == 0) & (g == 0))
    def _():
        wt_ref[...] = jnp.concatenate([_group_rows(_rounded(q_ref[0])),
                                       jnp.ones((n_heads * GROUP, LANES), F32)], axis=1).astype(BF16)
        m_ref[0] = jnp.broadcast_to(self_score(), (n_heads, LANES))
        l_ref[0] = jnp.ones((n_heads, LANES), F32)

    @pl.when((ph == 1) & (g == 0))
    def _():
        inv = 1.0 / l_ref[0]
        l_ref[0] = inv
        p_self = _own_lanes(jnp.exp2(self_score() - m_ref[0]) * inv, mask)
        acc_ref[0] = p_self * vn_ref[0]

    sel = sel_ref[...]

    def scores(i):
        bias = jnp.concatenate([(aux_ref[0, 2 * i + j][None] * sel).reshape(HALF_PAGE * n_heads, LANES).astype(BF16)
                                for j in range(2)], axis=0)
        return _page_scores(k_refs[i][0], bias, wt_ref[...], c)

    @pl.when(ph == 0)
    def _():
        for c0 in range(0, pp, PAGE_CHUNK):
            _stats_update([scores(i) for i in range(c0, c0 + PAGE_CHUNK)], [0.0] * PAGE_CHUNK, m_ref, l_ref, 0)

    @pl.when(ph == 1)
    def _():
        e_spread = _spread_matrix()
        m, inv, acc = m_ref[0], l_ref[0], acc_ref[0]
        for i in range(0, pp, 2):
            ps = [jnp.where(mask[None], jnp.exp2(scores(i + t) - m[None]) * inv[None], 0.0) for t in range(2)]
            acc = acc + _weigh_page_pair(ps, [v_refs[i][0], v_refs[i + 1][0]], e_spread)
        acc_ref[0] = acc

    @pl.when((ph == 1) & (g == pl.num_programs(2) - 1))
    def _():
        o_ref[0] = acc_ref[0]


def _page_specs(pp, n_pages, n_heads, base, second_phase_only):
    def index(b, ph, g, pt, i):
        step = ph * g if second_phase_only else g
        return (base + pt[b * n_pages + step * pp + i], 0, 0, 0)
    return [pl.BlockSpec((1, PAGE_SIZE, n_heads, HEAD_DIM), functools.partial(index, i=i)) for i in range(pp)]


def _lane_pick_table(n_heads):
    lane = np.arange(LANES)[None, :]
    k = np.arange(HALF_PAGE)[:, None]
    t = ((lane == k) | (lane == HALF_PAGE + k)).astype(np.float32)
    return jnp.asarray(np.broadcast_to(t[:, None, :], (HALF_PAGE, n_heads, LANES)))


def _tok_spec(n_heads, off):
    return pl.BlockSpec((1, n_heads, HEAD_DIM), lambda bi, ph, g, pt: (bi, off, 0))


def _decode_call(kernel, name, n_maps, head_args, head_specs, qa, kv, q_off, k_off, v_off, k_cache, v_cache,
                 page_table, base, n_heads, pp):
    b = qa.shape[0]
    n_pages = page_table.shape[1]
    assert n_pages % pp == 0 and pp % PAGE_CHUNK == 0 and pp % 2 == 0 and n_heads * GROUP == LANES
    state = pltpu.VMEM((n_maps, n_heads, LANES), F32)
    return pl.pallas_call(
        kernel,
        grid_spec=pltpu.PrefetchScalarGridSpec(
            num_scalar_prefetch=1, grid=(b, 2, n_pages // pp),
            in_specs=[_tok_spec(n_heads, q_off), _tok_spec(n_heads, k_off), _tok_spec(n_heads, v_off)]
                     + head_specs + _page_specs(pp, n_pages, n_heads, base, False)
                     + _page_specs(pp, n_pages, n_heads, base, True),
            out_specs=_tok_spec(n_heads, 0),
            scratch_shapes=[pltpu.VMEM((n_maps * n_heads * GROUP, 2 * LANES), BF16), state, state,
                            pltpu.VMEM((1, n_heads, LANES), F32)]),
        out_shape=jax.ShapeDtypeStruct((b, n_heads, HEAD_DIM), F32),
        compiler_params=_cparams("parallel", "arbitrary", "arbitrary"),
        name=name,
    )(page_table.reshape(-1), qa, kv, kv, *head_args, *([k_cache] * pp), *([v_cache] * pp))


def fox_decode(qa, kv, q_off, k_off, v_off, aux, k_cache, v_cache, page_table, base, n_heads, *, pp):
    specs = [pl.BlockSpec((1, 2 * pp, n_heads, LANES), lambda bi, ph, g, pt: (bi, g, 0, 0)),
             pl.BlockSpec((HALF_PAGE, n_heads, LANES), lambda bi, ph, g, pt: (0, 0, 0))]
    return _decode_call(functools.partial(_fox_decode_kernel, pp=pp), "fox_decode", 1,
                        [aux, _lane_pick_table(n_heads)], specs, qa, kv, q_off, k_off, v_off, k_cache, v_cache,
                        page_table, base, n_heads, pp)


def _diff_decode_kernel(pt_ref, q_ref, kn_ref, vn_ref, lq1_ref, lk1_ref, lq2_ref, lk2_ref, gsub_ref,
                        c0_ref, tab_ref, *rest, pp, lam_init):
    k_refs, v_refs = rest[:pp], rest[pp:2 * pp]
    o_ref, wt_ref, m_ref, l_ref, acc_ref = rest[2 * pp:]
    ph, g = pl.program_id(1), pl.program_id(2)
    n_steps = pl.num_programs(2)
    n_heads = q_ref.shape[1]
    c = DIFF_QK_DIM ** -0.5 * LOG2E
    mask = _group_mask(n_heads)
    first = lax.broadcasted_iota(jnp.int32, (n_heads, HEAD_DIM), 1) < DIFF_QK_DIM

    def q_halves():
        qb = _rounded(q_ref[0])
        return jnp.where(first, qb, 0.0), jnp.where(first, 0.0, qb)

    def self_scores():
        prod = q_ref[0] * kn_ref[0]
        return [jnp.sum(jnp.where(first, prod, 0.0), axis=-1, keepdims=True) * c,
                jnp.sum(jnp.where(first, 0.0, prod), axis=-1, keepdims=True) * c]

    @pl.when((ph == 0) & (g == 0))
    def _():
        wt_ref[...] = jnp.concatenate(
            [jnp.concatenate([_group_rows(qh) for qh in q_halves()], axis=0),
             jnp.ones((2 * n_heads * GROUP, LANES), F32)], axis=1).astype(BF16)
        for t, s_self in enumerate(self_scores()):
            m_ref[t] = jnp.broadcast_to(s_self, (n_heads, LANES))
            l_ref[t] = jnp.ones((n_heads, LANES), F32)

    @pl.when((ph == 1) & (g == 0))
    def _():
        lam = _diff_lambda(lq1_ref, lk1_ref, lq2_ref, lk2_ref, lam_init)
        ps = []
        for t, s_self in enumerate(self_scores()):
            inv = 1.0 / l_ref[t]
            l_ref[t] = inv
            ps.append(_own_lanes(jnp.exp2(s_self - m_ref[t]) * inv, mask))
        acc_ref[0] = (ps[0] - lam * ps[1]) * vn_ref[0]

    def scores(i):
        s = _page_scores(k_refs[i][0], tab_ref[...], wt_ref[...], c)
        pages_after = (n_steps * pp - 1 - (g * pp + i)).astype(F32)
        return s, c0_ref[...] * pages_after

    @pl.when(ph == 0)
    def _():
        for c0 in range(0, pp, PAGE_CHUNK):
            pages = [scores(i) for i in range(c0, c0 + PAGE_CHUNK)]
            for t in range(2):
                _stats_update([s[:, :, t * LANES:(t + 1) * LANES] for s, _ in pages], [off for _, off in pages],
                              m_ref, l_ref, t)

    @pl.when(ph == 1)
    def _():
        lam = _diff_lambda(lq1_ref, lk1_ref, lq2_ref, lk2_ref, lam_init)
        e_spread = _spread_matrix()
        acc = acc_ref[0]

        def weights(i):
            s, off = scores(i)
            p1, p2 = [jnp.exp2(s[:, :, t * LANES:(t + 1) * LANES] - (m_ref[t] - off)[None]) * l_ref[t][None]
                      for t in range(2)]
            return jnp.where(mask[None], p1 - lam * p2, 0.0)

        for i in range(0, pp, 2):
            acc = acc + _weigh_page_pair([weights(i), weights(i + 1)], [v_refs[i][0], v_refs[i + 1][0]], e_spread)
        acc_ref[0] = acc

    @pl.when((ph == 1) & (g == n_steps - 1))
    def _():
        o_ref[0] = _rms(acc_ref[0], gsub_ref[...]) * (1.0 - lam_init)


def _alibi_tables(n_heads):
    slopes = np.asarray([2.0 ** (-8.0 * (h + 1) / n_heads) for h in range(n_heads)], np.float32)
    inv_scale = float(DIFF_QK_DIM ** 0.5)
    tab = np.zeros((PAGE_SIZE, n_heads, LANES), np.float32)
    tab[:, :, 0] = -slopes[None, :] * inv_scale * (PAGE_SIZE - np.arange(PAGE_SIZE, dtype=np.float32))[:, None]
    assert np.array_equal(tab.astype(BF16).astype(np.float32), tab)
    c0 = np.broadcast_to((-slopes * PAGE_SIZE * LOG2E)[:, None], (n_heads, LANES)).astype(np.float32)
    return jnp.asarray(c0), jnp.asarray(tab.reshape(PAGE_SIZE * n_heads, LANES), BF16)


def diff_decode(qa, kv, q_off, k_off, v_off, lams, gsub, lam_init, k_cache, v_cache, page_table, base, n_heads,
                *, pp):
    small = lambda n: pl.BlockSpec((1, n), lambda bi, ph, g, pt: (0, 0))
    specs = [small(DIFF_QK_DIM), small(DIFF_QK_DIM), small(DIFF_QK_DIM), small(DIFF_QK_DIM), small(HEAD_DIM),
             pl.BlockSpec((n_heads, LANES), lambda bi, ph, g, pt: (0, 0)),
             pl.BlockSpec((PAGE_SIZE * n_heads, LANES), lambda bi, ph, g, pt: (0, 0))]
    return _decode_call(functools.partial(_diff_decode_kernel, pp=pp, lam_init=lam_init), "diff_decode", 2,
                        [*lams, gsub, *_alibi_tables(n_heads)], specs, qa, kv, q_off, k_off, v_off,
                        k_cache, v_cache, page_table, base, n_heads, pp)


def _decay_terms(cache_logf, page_table, slogf, base):
    b, n_pages = page_table.shape
    h = cache_logf.shape[-1]
    past = n_pages * PAGE_SIZE
    lf = cache_logf[base + page_table].reshape(b, past, h).astype(F32)
    dcum = jnp.cumsum(jnp.concatenate([lf, slogf], axis=1), axis=1)
    rel = (dcum[:, past:past + 1] - dcum[:, :past]) * (HEAD_DIM ** 0.5)
    hi = rel.astype(BF16).astype(F32)
    mid = (rel - hi).astype(BF16).astype(F32)
    both = jnp.stack([hi, mid], axis=1).reshape(b, 2, 2 * n_pages, HALF_PAGE, h)
    return both.transpose(0, 2, 4, 1, 3).reshape(b, 2 * n_pages, h, LANES)


def _pad_lanes(a):
    return jnp.pad(a, [(0, 0)] * (a.ndim - 1) + [(0, LANES - a.shape[-1])])


def kernel(x_prompt, x_sample, cache_fox_k, cache_fox_v, cache_fox_logf, cache_diff_k, cache_diff_v, cache_mem_k, cache_mem_v, page_table, mem_prompt, g_mix, w_in, b_forget, lambda_q1, lambda_k1, lambda_q2, lambda_k2, g_diff_sub, w_out, g_xattn, g_mem, w_mem_kv, w_xq, w_xo, g_ffn, w_router, b_router, w_gate_up, b_gate_up, w_down, b_down, g_final):
    bsz, seq, d = x_prompt.shape
    dbs, dec_seq, _ = x_sample.shape
    assert dec_seq == 1
    depth = g_mix.shape[0]
    n_pool = cache_fox_k.shape[1]
    hf = b_forget.shape[1]
    fox_w = hf * HEAD_DIM
    diff_w = (w_in.shape[2] - 3 * fox_w - hf) // 3
    hd = diff_w // HEAD_DIM
    assert fox_w == diff_w
    n_mem = mem_prompt.shape[1]
    xw = w_xq.shape[2]
    n_p, n_s = bsz * seq, dbs * dec_seq
    n_pages = page_table.shape[1]
    tm_p = min(512, n_p)
    t_attn = min(512, seq)
    pp = min(8, n_pages)

    fox_kc = cache_fox_k.reshape(depth * n_pool, PAGE_SIZE, hf, HEAD_DIM)
    fox_vc = cache_fox_v.reshape(depth * n_pool, PAGE_SIZE, hf, HEAD_DIM)
    diff_kc = cache_diff_k.reshape(depth * n_pool, PAGE_SIZE, hd, HEAD_DIM)
    diff_vc = cache_diff_v.reshape(depth * n_pool, PAGE_SIZE, hd, HEAD_DIM)
    logf_c = cache_fox_logf.reshape(depth * n_pool, PAGE_SIZE, hf)
    n_exp = w_gate_up.shape[1]
    wgu_all = w_gate_up.reshape(depth * n_exp, d, w_gate_up.shape[3])
    bgu_all = b_gate_up.reshape(depth * n_exp, -1)
    wd_all = w_down.reshape(depth * n_exp, w_down.shape[2], d)
    bd_all = b_down.reshape(depth * n_exp, d)

    q_groups = [(BF16, fox_w + diff_w), (F32, 2 * fox_w + 2 * diff_w)]
    col_scale = jnp.concatenate([jnp.full((fox_w,), HEAD_DIM ** -0.5 * LOG2E, F32),
                                 jnp.full((diff_w,), DIFF_QK_DIM ** -0.5 * LOG2E, F32),
                                 jnp.ones((2 * fox_w + 2 * diff_w,), F32)]).reshape(1, -1)

    xp = x_prompt.reshape(n_p, d)
    xs = x_sample.reshape(n_s, d)
    outs = [[] for _ in range(12)]
    for l in range(depth):
        lam_init = 0.8 - 0.6 * math.exp(-0.3 * l)
        lams = tuple(a[l].reshape(1, DIFF_QK_DIM) for a in (lambda_q1, lambda_k1, lambda_q2, lambda_k2))
        gsub = g_diff_sub[l].reshape(1, HEAD_DIM)
        wl = w_in[l]
        c_fg, c_dq = 3 * fox_w, 3 * fox_w + hf
        w_main = jnp.concatenate([wl[:, :fox_w], wl[:, c_dq:c_dq + diff_w], wl[:, fox_w:c_fg],
                                  wl[:, c_dq + diff_w:]], axis=1)
        w_fg = _pad_lanes(wl[:, c_fg:c_dq])
        b_fg = _pad_lanes(b_forget[l].reshape(1, hf))
        w_r = _pad_lanes(w_router[l])
        b_r = _pad_lanes(b_router[l].reshape(1, n_exp))
        last = l == depth - 1

        qa, kv, fg = norm_project(xp, g_mix[l], w_main.astype(BF16), q_groups, col_scale, w_fg.astype(BF16),
                                  tm=min(1024, n_p), tn=512)
        logf, dcum = logf_cumsum(fg.reshape(bsz, seq, LANES), b_fg, True)
        qa3, kv3 = qa.reshape(bsz, seq, -1), kv.reshape(bsz, seq, -1)
        fox_o = fox_prompt(qa3, kv3, dcum, hf, 0, 0, hf, t=t_attn)
        diff_o = diff_prompt(qa3, kv3, lams, gsub, lam_init, hd, hf, 2 * hf, 2 * hf + hd, t=t_attn)
        xp = out_project(xp, fox_o.reshape(n_p, fox_w), diff_o.reshape(n_p, diff_w), w_out[l].astype(BF16),
                         tm=tm_p, tn=d)
        mk, mv = norm_project(mem_prompt.reshape(bsz * n_mem, d), g_mem[l], w_mem_kv[l].astype(BF16),
                              [(F32, xw), (F32, xw)], tm=min(512, bsz * n_mem), tn=xw)
        xp = cross_attend(xp.reshape(bsz, seq, d), g_xattn[l], w_xq[l].astype(BF16), mk.reshape(bsz, n_mem, xw),
                          mv.reshape(bsz, n_mem, xw), w_xo[l].astype(BF16), tm=min(512, seq)).reshape(n_p, d)
        hp_p, e_p, gate_p, rank_p, cnt_p = route(xp, g_ffn[l], w_r.astype(BF16), b_r,
                                                 jnp.zeros((1, LANES), F32), tm=tm_p)

        sqa, skv, sfg = norm_project(xs, g_mix[l], w_main.astype(BF16), [(F32, wd) for _, wd in q_groups], None,
                                     w_fg.astype(BF16), tm=n_s, tn=512)
        slogf, = logf_cumsum(sfg.reshape(dbs, 1, LANES), b_fg, False)
        sqa3, skv3 = sqa.reshape(dbs, hf + hd, HEAD_DIM), skv.reshape(dbs, 2 * hf + 2 * hd, HEAD_DIM)
        aux = _decay_terms(logf_c, page_table, slogf[:, :, :hf], l * n_pool)
        fox_os = fox_decode(sqa3, skv3, 0, 0, 1, aux, fox_kc, fox_vc, page_table, l * n_pool, hf, pp=pp)
        diff_os = diff_decode(sqa3, skv3, 1, 2, 3, lams, gsub, lam_init, diff_kc, diff_vc, page_table,
                              l * n_pool, hd, pp=pp)
        xs = out_project(xs, fox_os.reshape(n_s, fox_w), diff_os.reshape(n_s, diff_w), w_out[l].astype(BF16),
                         tm=n_s, tn=d)
        xs = cross_attend(xs.reshape(dbs, 1, d), g_xattn[l], w_xq[l].astype(BF16),
                          cache_mem_k[l].reshape(dbs, n_mem, xw), cache_mem_v[l].reshape(dbs, n_mem, xw),
                          w_xo[l].astype(BF16), tm=1).reshape(n_s, d)
        hp_s, e_s, gate_s, rank_s, cnt = route(xs, g_ffn[l], w_r.astype(BF16), b_r, cnt_p, tm=n_s)

        counts = cnt[0, :n_exp].astype(jnp.int32)
        row_start, block_expert, n_used, nb = _block_tables(counts, (n_p + n_s) * TOP_K, MOE_TM)
        slots_p = row_start[e_p[:, :TOP_K]] + rank_p[:, :TOP_K]
        slots_s = row_start[e_s[:, :TOP_K]] + rank_s[:, :TOP_K]
        x_sorted = jnp.zeros((nb * MOE_TM, d // 2), U32)
        x_sorted = moe_dispatch(hp_p, slots_p, x_sorted, td=min(256, n_p))
        x_sorted = moe_dispatch(hp_s, slots_s, x_sorted, td=n_s)
        block_expert = block_expert + l * n_exp
        a_sorted = moe_up(x_sorted, wgu_all, bgu_all, block_expert, n_used, tm=MOE_TM, tn=512)
        o_sorted = moe_down(a_sorted, wd_all, bd_all, block_expert, n_used, tm=MOE_TM, tn=1024)
        xp = moe_combine(xp, gate_p, slots_p, o_sorted, g_final, tc=min(128, n_p), final_norm=last)
        xs = moe_combine(xs, gate_s, slots_s, o_sorted, g_final, tc=n_s, final_norm=last)

        heads = lambda a, b_, n, off: a[:, off * HEAD_DIM:(off + n) * HEAD_DIM].reshape(b_, -1, n, HEAD_DIM)
        new = (heads(kv, bsz, hf, 0), heads(kv, bsz, hf, hf), logf[:, :, :hf],
               heads(kv, bsz, hd, 2 * hf), heads(kv, bsz, hd, 2 * hf + hd),
               mk.reshape(bsz, n_mem, N_XATTN_HEADS, xw // N_XATTN_HEADS),
               mv.reshape(bsz, n_mem, N_XATTN_HEADS, xw // N_XATTN_HEADS),
               heads(skv, dbs, hf, 0), heads(skv, dbs, hf, hf), slogf[:, :, :hf],
               heads(skv, dbs, hd, 2 * hf), heads(skv, dbs, hd, 2 * hf + hd))
        for acc, a in zip(outs, new):
            acc.append(a)

    return (xp.reshape(bsz, seq, d), xs.reshape(dbs, dec_seq, d)) + tuple(jnp.stack(a) for a in outs)
```
